```python
import jax
import jax.numpy as jnp
from jax import lax
import numpy as np

D_MODEL = 4096
BATCH = 16
SEQ = 256
DEPTH = 1
DEC_BATCH = 8
DEC_SEQ = 4096
PAST_LEN = 512

GRID_W = 64
MIX_WIDTH = D_MODEL
M_WIDTH = MIX_WIDTH // 2
F_WIDTH = MIX_WIDTH - M_WIDTH
M_HEADS = 4
M_DV = M_WIDTH // M_HEADS
M_DQK = M_DV // 2
F_GROUPS = 8
F_CG = F_WIDTH // F_GROUPS
N_GATES = 4 * M_HEADS
N_IN = 2 * M_HEADS * M_DQK + 3 * M_WIDTH + N_GATES + 2 * F_WIDTH
CHUNK = 64
EPS = 1e-6

kernel_name = 'hybrid_mlstm_fnet_prefix_diffusion_step'


def _rmsnorm(x, w):
    xf = x.astype(jnp.float32)
    y = xf * lax.rsqrt(jnp.mean(xf * xf, axis=-1, keepdims=True) + EPS)
    return (y * w.astype(jnp.float32)).astype(x.dtype)


def _split_in(proj):
    sizes = (M_HEADS * M_DQK, M_HEADS * M_DQK, M_WIDTH, M_WIDTH, M_WIDTH, N_GATES, F_WIDTH, F_WIDTH)
    outs = []
    start = 0
    for s in sizes:
        outs.append(proj[..., start:start + s])
        start += s
    return outs


def _modulation(cvec, w_ada, b_ada):
    mod = jax.nn.silu(cvec) @ w_ada + b_ada
    shift, scale, gate = jnp.split(mod, 3, axis=-1)
    return shift, scale, gate


def _mlstm_chunkwise(q, k, v, ig, lf, C0, n0, m0):
    bsz, nh, T, _ = q.shape
    dv = v.shape[-1]
    nc = T // CHUNK

    def to_chunks(a):
        a = a.reshape((bsz, nh, nc, CHUNK) + a.shape[3:])
        return jnp.moveaxis(a, 2, 0)

    causal = jnp.tril(jnp.ones((CHUNK, CHUNK), dtype=bool))

    def step(carry, inp):
        C, n, m = carry
        qb, kb, vb, ib, fb = inp
        b = jnp.cumsum(fb, axis=-1)
        a = b + m[..., None]
        logw = jnp.where(causal, b[..., :, None] - b[..., None, :] + ib[..., None, :], -jnp.inf)
        mt = jnp.maximum(a, jnp.max(logw, axis=-1))
        s = jnp.einsum('bhtd,bhsd->bhts', qb, kb) * jnp.exp(logw - mt[..., None])
        inter = jnp.exp(a - mt)
        num = jnp.einsum('bhts,bhse->bhte', s, vb) + inter[..., None] * jnp.einsum('bhtd,bhde->bhte', qb, C)
        den = jnp.sum(s, axis=-1) + inter * jnp.einsum('bhtd,bhd->bht', qb, n)
        h = num / jnp.maximum(jnp.abs(den), jnp.exp(-mt))[..., None]
        bl = b[..., -1]
        g = bl[..., None] - b + ib
        m_new = jnp.maximum(bl + m, jnp.max(g, axis=-1))
        decay = jnp.exp(bl + m - m_new)
        kw = kb * jnp.exp(g - m_new[..., None])[..., None]
        C_new = decay[..., None, None] * C + jnp.einsum('bhsd,bhse->bhde', kw, vb)
        n_new = decay[..., None] * n + jnp.sum(kw, axis=2)
        return (C_new, n_new, m_new), h

    carry0 = (C0.astype(jnp.float32), n0.astype(jnp.float32), m0.astype(jnp.float32))
    xs = (to_chunks(q), to_chunks(k), to_chunks(v), to_chunks(ig), to_chunks(lf))
    (C, n, m), hc = lax.scan(step, carry0, xs)
    h = jnp.moveaxis(hc, 0, 2).reshape(bsz, nh, T, dv)
    return h, (C, n, m)


def _mixer_layer(x, shift, scale, gate, norm_w, w_in, b_gates, hnorm_w, w_four, w_out,
                 init_fwd, init_bwd, grid_rows):
    bsz, T, _ = x.shape
    f32 = jnp.float32
    h = (_rmsnorm(x, norm_w) * (1 + scale[:, None, :]) + shift[:, None, :]).astype(x.dtype)
    proj = h @ w_in
    q, k, v, o, z_m, gt, u, z_f = _split_in(proj)

    def heads(a, d):
        return a.reshape(bsz, T, M_HEADS, d).transpose(0, 2, 1, 3).astype(f32)
    q = heads(q, M_DQK)
    k = heads(k, M_DQK) * (M_DQK ** -0.5)
    v = heads(v, M_DV)
    gt = (gt.astype(f32) + b_gates.astype(f32)).reshape(bsz, T, 4, M_HEADS).transpose(2, 0, 3, 1)
    ig_f, lf_f = gt[0], jax.nn.log_sigmoid(gt[1])
    ig_b, lf_b = gt[2], jax.nn.log_sigmoid(gt[3])
    h_f, st_f = _mlstm_chunkwise(q, k, v, ig_f, lf_f, *init_fwd)

    def rev(a):
        return jnp.flip(a, axis=2)
    h_b, st_b = _mlstm_chunkwise(rev(q), rev(k), rev(v), rev(ig_b), rev(lf_b), *init_bwd)
    hm = h_f + rev(h_b)
    hm = hm * lax.rsqrt(jnp.mean(hm * hm, axis=-1, keepdims=True) + EPS)
    hm = hm.transpose(0, 2, 1, 3).reshape(bsz, T, M_WIDTH)
    m_out = (hm * hnorm_w.astype(f32) * jax.nn.sigmoid(o.astype(f32))).astype(x.dtype) * jax.nn.silu(z_m)

    uf = u.astype(f32)
    if grid_rows is None:
        uf = uf.reshape(bsz, T, F_GROUPS, F_CG)
        mixed = jnp.real(jnp.fft.fftn(uf, axes=(1, 3), norm='ortho'))
    else:
        uf = uf.reshape(bsz, grid_rows, GRID_W, F_GROUPS, F_CG)
        mixed = jnp.real(jnp.fft.fftn(uf, axes=(1, 2, 4), norm='ortho')).reshape(bsz, T, F_GROUPS, F_CG)
    f_out = jnp.einsum('btgc,gcd->btgd', mixed.astype(x.dtype), w_four).reshape(bsz, T, F_WIDTH)
    f_out = f_out * jax.nn.silu(z_f)

    out = jnp.concatenate([m_out, f_out], axis=-1) @ w_out
    y = (x + gate[:, None, :] * out).astype(x.dtype)
    return y, st_f, st_b


def setup_inputs(seed: int = 0) -> dict:
    key = jax.random.key(seed)
    ks = jax.random.split(key, 16)
    nrm = jax.random.normal
    f32 = jnp.float32
    x_prompt = nrm(ks[0], (BATCH, SEQ, D_MODEL), f32)
    x_sample = nrm(ks[1], (DEC_BATCH, DEC_SEQ, D_MODEL), f32)
    c = nrm(ks[2], (DEC_BATCH, D_MODEL), f32)
    state_C = 0.05 * nrm(ks[3], (DEC_BATCH, DEPTH, 2, M_HEADS, M_DQK, M_DV), f32)
    state_n = 0.5 * nrm(ks[4], (DEC_BATCH, DEPTH, 2, M_HEADS, M_DQK), f32)
    state_m = 0.5 * nrm(ks[5], (DEC_BATCH, DEPTH, 2, M_HEADS), f32)
    c_ctx = nrm(ks[6], (D_MODEL,), f32)
    w_ada = 0.5 * (D_MODEL ** -0.5) * nrm(ks[7], (DEPTH, D_MODEL, 3 * D_MODEL), f32)
    b_ada = 0.02 * nrm(ks[8], (DEPTH, 3 * D_MODEL), f32)
    norm_w = 1.0 + 0.02 * nrm(ks[9], (DEPTH, D_MODEL), f32)
    w_in = (D_MODEL ** -0.5) * nrm(ks[10], (DEPTH, D_MODEL, N_IN), f32)
    f_bias = jnp.linspace(3.0, 6.0, M_HEADS, dtype=f32)
    i_bias = jnp.zeros((M_HEADS,), f32)
    base = jnp.concatenate([i_bias, f_bias, i_bias, f_bias])
    b_gates = base[None, :] + 0.1 * nrm(ks[11], (DEPTH, N_GATES), f32)
    hnorm_w = 1.0 + 0.02 * nrm(ks[12], (DEPTH, M_WIDTH), f32)
    w_four = (F_CG ** -0.5) * nrm(ks[13], (DEPTH, F_GROUPS, F_CG, F_CG), f32)
    w_out = (MIX_WIDTH ** -0.5) * nrm(ks[14], (DEPTH, MIX_WIDTH, D_MODEL), f32)
    final_norm_w = 1.0 + 0.02 * nrm(ks[15], (D_MODEL,), f32)
    return {'x_prompt': x_prompt, 'x_sample': x_sample, 'c': c,
            'state_C': state_C, 'state_n': state_n, 'state_m': state_m,
            'c_ctx': c_ctx, 'w_ada': w_ada, 'b_ada': b_ada, 'norm_w': norm_w,
            'w_in': w_in, 'b_gates': b_gates, 'hnorm_w': hnorm_w, 'w_four': w_four,
            'w_out': w_out, 'final_norm_w': final_norm_w}


def reference(x_prompt, x_sample, c, state_C, state_n, state_m, c_ctx, w_ada, b_ada, norm_w,
              w_in, b_gates, hnorm_w, w_four, w_out, final_norm_w):
    f32 = jnp.float32
    bp = x_prompt.shape[0]
    rows = x_sample.shape[1] // GRID_W
    zero_state = (jnp.zeros((bp, M_HEADS, M_DQK, M_DV), f32),
                  jnp.zeros((bp, M_HEADS, M_DQK), f32),
                  jnp.zeros((bp, M_HEADS), f32))
    xp = x_prompt
    xs = x_sample
    Cs, ns, ms = [], [], []
    for l in range(DEPTH):
        sh_p, sc_p, g_p = _modulation(c_ctx[None, :], w_ada[l], b_ada[l])
        xp, st_f, st_b = _mixer_layer(xp, sh_p, sc_p, g_p, norm_w[l], w_in[l], b_gates[l], hnorm_w[l],
                                      w_four[l], w_out[l], zero_state, zero_state, None)
        Cs.append(jnp.stack([st_f[0], st_b[0]], axis=1))
        ns.append(jnp.stack([st_f[1], st_b[1]], axis=1))
        ms.append(jnp.stack([st_f[2], st_b[2]], axis=1))
        sh_s, sc_s, g_s = _modulation(c, w_ada[l], b_ada[l])
        init_f = (state_C[:, l, 0], state_n[:, l, 0], state_m[:, l, 0])
        init_b = (state_C[:, l, 1], state_n[:, l, 1], state_m[:, l, 1])
        xs, _, _ = _mixer_layer(xs, sh_s, sc_s, g_s, norm_w[l], w_in[l], b_gates[l], hnorm_w[l],
                                w_four[l], w_out[l], init_f, init_b, rows)
    y_prompt = _rmsnorm(xp, final_norm_w)
    y_sample = _rmsnorm(xs, final_norm_w)
    new_C = jnp.stack(Cs, axis=1)
    new_n = jnp.stack(ns, axis=1)
    new_m = jnp.stack(ms, axis=1)
    return (y_prompt, y_sample, new_C, new_n, new_m)
```

```python
import functools

import numpy as np
import jax
import jax.numpy as jnp
from jax import lax
from jax.experimental import pallas as pl
from jax.experimental.pallas import tpu as pltpu

F32 = jnp.float32
BF16 = jnp.bfloat16

EPS = 1e-6
GRID_W = 64
MLSTM_CHUNK = 256
GATE_SLOTS = 8
MOD_ROWS = 16
V7X_VMEM_BYTES = 64 * 1024 * 1024
VMEM_LIMIT = V7X_VMEM_BYTES - 8 * 1024 * 1024


def _params(*sem):
    return pltpu.CompilerParams(dimension_semantics=sem, vmem_limit_bytes=VMEM_LIMIT)


def _log_sigmoid(x):
    return jnp.minimum(x, 0.0) - jnp.log1p(jnp.exp(-jnp.abs(x)))


def _silu(x):
    return x * jax.nn.sigmoid(x)


def _mod_body(c_ref, w_ref, b_ref, o_ref):
    s = _silu(c_ref[...]).astype(BF16)
    o_ref[...] = jnp.dot(s, w_ref[...].astype(BF16), preferred_element_type=F32) + b_ref[...]


def _modulation(cvec, w_ada, b_ada):
    rows, d = cvec.shape
    n = w_ada.shape[1]
    tn = min(512, n)
    return pl.pallas_call(
        _mod_body,
        grid=(n // tn,),
        in_specs=[pl.BlockSpec((rows, d), lambda j: (0, 0)),
                  pl.BlockSpec((d, tn), lambda j: (0, j)),
                  pl.BlockSpec((1, tn), lambda j: (0, j))],
        out_specs=pl.BlockSpec((rows, tn), lambda j: (0, j)),
        out_shape=jax.ShapeDtypeStruct((rows, n), F32),
        compiler_params=_params("arbitrary"),
        name="modulation",
    )(cvec, w_ada, b_ada)


def _prenorm_body(x_ref, sh_ref, sc_ref, nw_ref, wg_ref, wgt_ref, bg_ref, bgt_ref,
                  h_ref, gc_ref, gr_ref, *, heads, sub):
    tm = x_ref.shape[0]
    nw = nw_ref[...]
    scale1 = 1.0 + sc_ref[...]
    shift = sh_ref[...]

    def rows_step(i, carry):
        r0 = pl.multiple_of(i * sub, sub)
        x = x_ref[pl.ds(r0, sub), :]
        ms = jnp.mean(x * x, axis=-1, keepdims=True)
        y = x * lax.rsqrt(ms + EPS) * nw
        h_ref[pl.ds(r0, sub), :] = (y * scale1 + shift).astype(BF16)
        return carry

    lax.fori_loop(0, tm // sub, rows_step, 0)
    hb = h_ref[...]
    g = jnp.dot(hb, wg_ref[...], preferred_element_type=F32) + bg_ref[...]
    col = lax.broadcasted_iota(jnp.int32, g.shape, 1) % GATE_SLOTS
    g = jnp.where((col == 1) | (col == 3), _log_sigmoid(g), g)
    for hd in range(heads):
        gc_ref[hd] = g[:, hd * GATE_SLOTS:(hd + 1) * GATE_SLOTS]
    gt = lax.dot_general(wgt_ref[...], hb, (((1,), (1,)), ((), ())),
                         preferred_element_type=F32) + bgt_ref[...]
    row = lax.broadcasted_iota(jnp.int32, gt.shape, 0) % GATE_SLOTS
    gr_ref[...] = jnp.where((row == 1) | (row == 3), _log_sigmoid(gt), gt)


def _prenorm(x, mod, mod_row, norm_w, wg, wgt, bg, bgt, heads):
    bsz, t, d = x.shape
    tm = min(512, t)
    gw = heads * GATE_SLOTS
    body = functools.partial(_prenorm_body, heads=heads, sub=min(64, tm))
    return pl.pallas_call(
        body,
        grid=(bsz, t // tm),
        in_specs=[pl.BlockSpec((None, tm, d), lambda b, i: (b, i, 0)),
                  pl.BlockSpec((None, 1, d), lambda b, i: (mod_row(b) * 3, 0, 0)),
                  pl.BlockSpec((None, 1, d), lambda b, i: (mod_row(b) * 3 + 1, 0, 0)),
                  pl.BlockSpec((1, d), lambda b, i: (0, 0)),
                  pl.BlockSpec((d, gw), lambda b, i: (0, 0)),
                  pl.BlockSpec((gw, d), lambda b, i: (0, 0)),
                  pl.BlockSpec((1, gw), lambda b, i: (0, 0)),
                  pl.BlockSpec((gw, 1), lambda b, i: (0, 0))],
        out_specs=[pl.BlockSpec((None, tm, d), lambda b, i: (b, i, 0)),
                   pl.BlockSpec((None, heads, tm, GATE_SLOTS), lambda b, i: (b, 0, i, 0)),
                   pl.BlockSpec((None, gw, tm), lambda b, i: (b, 0, i))],
        out_shape=[jax.ShapeDtypeStruct((bsz, t, d), BF16),
                   jax.ShapeDtypeStruct((bsz, heads, t, GATE_SLOTS), F32),
                   jax.ShapeDtypeStruct((bsz, gw, t), F32)],
        compiler_params=_params("arbitrary", "arbitrary"),
        name="prenorm",
    )(x, mod, mod, norm_w, wg, wgt, bg, bgt)


def _matmul_body(a_ref, w_ref, o_ref):
    o_ref[...] = jnp.dot(a_ref[...], w_ref[...], preferred_element_type=F32).astype(o_ref.dtype)


def _in_proj(h, w):
    m, k = h.shape
    n = w.shape[1]
    tm, tn = min(1024, m), min(1024, n)
    return pl.pallas_call(
        _matmul_body,
        grid=(m // tm, n // tn),
        in_specs=[pl.BlockSpec((tm, k), lambda i, j: (i, 0)),
                  pl.BlockSpec((k, tn), lambda i, j: (0, j))],
        out_specs=pl.BlockSpec((tm, tn), lambda i, j: (i, j)),
        out_shape=jax.ShapeDtypeStruct((m, n), BF16),
        compiler_params=_params("arbitrary", "arbitrary"),
        name="in_proj",
    )(h, w)


def _mlstm_body(*refs, nc, has_init, has_state_out, k_scale):
    q_ref, k_ref, v_ref, o_ref, z_ref, gc_ref, gr_ref, hw_ref = refs[:8]
    pos = 8
    if has_init:
        c0_ref, n0_ref, m0_ref = refs[pos:pos + 3]
        pos += 3
    out_ref = refs[pos]
    pos += 1
    if has_state_out:
        cn_ref, nn_ref, mn_ref = refs[pos:pos + 3]
        pos += 3
    c_scr, n_scr, m_scr, hf_scr = refs[pos:]
    step = pl.program_id(2)
    chunk = q_ref.shape[0]

    def run(d):
        first = step == (0 if d == 0 else nc)
        last = step == (nc - 1 if d == 0 else 2 * nc - 1)
        ci = step if d == 0 else 2 * nc - 1 - step
        r0 = pl.multiple_of(ci * chunk, chunk)

        @pl.when(first)
        def _():
            if has_init:
                c_scr[...] = c0_ref[d]
                n_scr[...] = n0_ref[d]
                m_scr[...] = m0_ref[d]
            else:
                c_scr[...] = jnp.zeros_like(c_scr)
                n_scr[...] = jnp.zeros_like(n_scr)
                m_scr[...] = jnp.zeros_like(m_scr)

        q = q_ref[...]
        ks = k_ref[...] * jnp.asarray(k_scale, BF16)
        v = v_ref[...]
        g = gc_ref[...]
        gt = gr_ref[...]
        i_col, lf_col = g[:, 2 * d:2 * d + 1], g[:, 2 * d + 1:2 * d + 2]
        i_row, lf_row = gt[2 * d:2 * d + 1, :], gt[2 * d + 1:2 * d + 2, :]
        t_i = lax.broadcasted_iota(jnp.int32, (chunk, chunk), 0)
        s_i = lax.broadcasted_iota(jnp.int32, (chunk, chunk), 1)
        visible = (s_i <= t_i) if d == 0 else (s_i >= t_i)
        visible_t = (t_i <= s_i) if d == 0 else (t_i >= s_i)
        b_col = jnp.sum(jnp.where(visible, lf_row, 0.0), axis=1, keepdims=True)
        b_row = jnp.sum(jnp.where(visible_t, lf_col, 0.0), axis=0, keepdims=True)
        u_row = i_row - b_row
        u_col = i_col - b_col
        b_all = jnp.sum(lf_row, axis=1, keepdims=True)
        m_old = m_scr[...]
        c_old = c_scr[...]
        n_old = n_scr[...]

        logw = jnp.where(visible, b_col + u_row, -jnp.inf)
        a = b_col + m_old
        mt = jnp.maximum(a, jnp.max(logw, axis=1, keepdims=True))
        qk = lax.dot_general(q, ks, (((1,), (1,)), ((), ())), preferred_element_type=F32)
        sm = qk * jnp.exp(logw - mt)
        inter = jnp.exp(a - mt)
        num = (jnp.dot(sm.astype(BF16), v, preferred_element_type=F32)
               + inter * jnp.dot(q, c_old.astype(BF16), preferred_element_type=F32))
        qn = jnp.sum(q.astype(F32) * n_old, axis=1, keepdims=True)
        den = jnp.sum(sm, axis=1, keepdims=True) + inter * qn
        h = num / jnp.maximum(jnp.abs(den), jnp.exp(-mt))

        m_new = jnp.maximum(b_all + m_old, b_all + jnp.max(u_row, axis=1, keepdims=True))
        decay = jnp.exp(b_all + m_old - m_new)
        kw = ks.astype(F32) * jnp.exp(b_all + u_col - m_new)
        c_new = decay * c_old + lax.dot_general(kw.astype(BF16), v, (((0,), (0,)), ((), ())),
                                                preferred_element_type=F32)
        n_new = decay * n_old + jnp.sum(kw, axis=0, keepdims=True)
        c_scr[...] = c_new
        n_scr[...] = n_new
        m_scr[...] = m_new

        if has_state_out:
            @pl.when(last)
            def _():
                cn_ref[d] = c_new
                nn_ref[d] = n_new
                mn_ref[d] = m_new

        if d == 0:
            hf_scr[pl.ds(r0, chunk), :] = h
        else:
            hm = hf_scr[pl.ds(r0, chunk), :] + h
            hm = hm * lax.rsqrt(jnp.mean(hm * hm, axis=-1, keepdims=True) + EPS)
            gate = jax.nn.sigmoid(o_ref[...].astype(F32))
            out_ref[...] = (hm * hw_ref[...] * gate * _silu(z_ref[...].astype(F32))).astype(out_ref.dtype)

    @pl.when(step < nc)
    def _():
        run(0)

    @pl.when(step >= nc)
    def _():
        run(1)


def _mlstm(proj, gc, gr, hnorm_w, init, heads, dqk, dv, want_state):
    bsz, t, _ = proj.shape
    chunk = min(MLSTM_CHUNK, t)
    nc = t // chunk

    def ci(s):
        return jnp.where(s < nc, s, 2 * nc - 1 - s)

    def co(s):
        return jnp.where(s < nc, nc - 1, 2 * nc - 1 - s)

    in_specs = [
        pl.BlockSpec((None, chunk, dqk), lambda b, h, s: (b, ci(s), h)),
        pl.BlockSpec((None, chunk, dqk), lambda b, h, s: (b, ci(s), heads + h)),
        pl.BlockSpec((None, chunk, dv), lambda b, h, s: (b, ci(s), heads + h)),
        pl.BlockSpec((None, chunk, dv), lambda b, h, s: (b, co(s), 2 * heads + h)),
        pl.BlockSpec((None, chunk, dv), lambda b, h, s: (b, co(s), 3 * heads + h)),
        pl.BlockSpec((None, None, chunk, GATE_SLOTS), lambda b, h, s: (b, h, ci(s), 0)),
        pl.BlockSpec((None, GATE_SLOTS, chunk), lambda b, h, s: (b, h, ci(s))),
        pl.BlockSpec((1, dv), lambda b, h, s: (0, h)),
    ]
    args = [proj, proj, proj, proj, proj, gc, gr, hnorm_w]
    if init is not None:
        c0, n0, m0 = init
        in_specs += [
            pl.BlockSpec((None, 2, None, dqk, dv), lambda b, h, s: (b, 0, h, 0, 0)),
            pl.BlockSpec((None, 2, None, 1, dqk), lambda b, h, s: (b, 0, h, 0, 0)),
            pl.BlockSpec((None, 2, None, 1, 1), lambda b, h, s: (b, 0, h, 0, 0)),
        ]
        args += [c0, n0, m0]
    out_specs = [pl.BlockSpec((None, chunk, dv), lambda b, h, s: (b, co(s), h))]
    out_shape = [jax.ShapeDtypeStruct((bsz, t, heads * dv), BF16)]
    if want_state:
        out_specs += [
            pl.BlockSpec((None, 2, None, dqk, dv), lambda b, h, s: (b, 0, h, 0, 0)),
            pl.BlockSpec((None, 2, None, 1, dqk), lambda b, h, s: (b, 0, h, 0, 0)),
            pl.BlockSpec((None, 2, None, 1, 1), lambda b, h, s: (b, 0, h, 0, 0)),
        ]
        out_shape += [
            jax.ShapeDtypeStruct((bsz, 2, heads, dqk, dv), F32),
            jax.ShapeDtypeStruct((bsz, 2, heads, 1, dqk), F32),
            jax.ShapeDtypeStruct((bsz, 2, heads, 1, 1), F32),
        ]
    body = functools.partial(_mlstm_body, nc=nc, has_init=init is not None,
                             has_state_out=want_state, k_scale=dqk ** -0.5)
    return pl.pallas_call(
        body,
        grid=(bsz, heads, 2 * nc),
        in_specs=in_specs,
        out_specs=out_specs,
        out_shape=out_shape,
        scratch_shapes=[pltpu.VMEM((dqk, dv), F32), pltpu.VMEM((1, dqk), F32),
                        pltpu.VMEM((1, 1), F32), pltpu.VMEM((t, dv), F32)],
        compiler_params=_params("arbitrary", "arbitrary", "arbitrary"),
        name="mlstm",
    )(*args)


def _dft_cos_sin(n):
    ang = 2.0 * np.pi * np.outer(np.arange(n), np.arange(n)) / n
    return np.cos(ang), np.sin(ang)


def _four_ctx_body(u_ref, z_ref, cc_ref, sc_ref, ct_ref, st_ref, w4_ref, o_ref, *, scale):
    x = u_ref[...]
    a = jnp.dot(x, cc_ref[...], preferred_element_type=F32).astype(BF16)
    b = jnp.dot(x, sc_ref[...], preferred_element_type=F32).astype(BF16)
    mixed = (jnp.dot(ct_ref[...], a, preferred_element_type=F32)
             - jnp.dot(st_ref[...], b, preferred_element_type=F32)) * scale
    f = jnp.dot(mixed.astype(BF16), w4_ref[...], preferred_element_type=F32)
    o_ref[...] = (f * _silu(z_ref[...].astype(F32))).astype(o_ref.dtype)


def _fourier_ctx(proj, w4, groups, cg, u_blk, z_blk):
    bsz, t, _ = proj.shape
    cc, sc = _dft_cos_sin(cg)
    ct, st = _dft_cos_sin(t)
    consts = [jnp.asarray(m, F32).astype(BF16) for m in (cc, sc, ct, st)]
    body = functools.partial(_four_ctx_body, scale=float(1.0 / np.sqrt(t * cg)))
    sq = lambda n: pl.BlockSpec((n, n), lambda b, g: (0, 0))
    return pl.pallas_call(
        body,
        grid=(bsz, groups),
        in_specs=[pl.BlockSpec((None, t, cg), lambda b, g: (b, 0, u_blk + g)),
                  pl.BlockSpec((None, t, cg), lambda b, g: (b, 0, z_blk + g)),
                  sq(cg), sq(cg), sq(t), sq(t),
                  pl.BlockSpec((None, cg, cg), lambda b, g: (g, 0, 0))],
        out_specs=pl.BlockSpec((None, t, cg), lambda b, g: (b, 0, g)),
        out_shape=jax.ShapeDtypeStruct((bsz, t, groups * cg), BF16),
        compiler_params=_params("arbitrary", "arbitrary"),
        name="fourier_ctx",
    )(proj, proj, *consts, w4)


def _four_lat_body(u_ref, z_ref, cc_ref, sc_ref, wp_ref, wq_ref, kr_ref, w4_ref, o_ref,
                   p_scr, q_scr, *, scale, tile, oct_w):
    rows, width, cg = p_scr.shape
    t = rows * width
    rows_per_tile = tile // width

    def width_stage(i, carry):
        r0 = pl.multiple_of(i * tile, tile)
        x = u_ref[pl.ds(r0, tile), :]
        a = jnp.dot(x, cc_ref[...], preferred_element_type=F32).astype(BF16)
        b = jnp.dot(x, sc_ref[...], preferred_element_type=F32).astype(BF16)
        ab = jnp.concatenate([a, b], axis=0)
        p = jnp.dot(wp_ref[...], ab, preferred_element_type=F32)
        q = jnp.dot(wq_ref[...], ab, preferred_element_type=F32)
        g0 = pl.multiple_of(i * rows_per_tile, rows_per_tile)
        p_scr[pl.ds(g0, rows_per_tile), :, :] = p.reshape(rows_per_tile, width, cg)
        q_scr[pl.ds(g0, rows_per_tile), :, :] = q.reshape(rows_per_tile, width, cg)
        return carry

    lax.fori_loop(0, t // tile, width_stage, 0)

    def row_stage(j, carry):
        w0 = pl.multiple_of(j * oct_w, oct_w)
        pj = p_scr[:, pl.ds(w0, oct_w), :].reshape(rows * oct_w, cg).astype(BF16)
        qj = q_scr[:, pl.ds(w0, oct_w), :].reshape(rows * oct_w, cg).astype(BF16)
        pq = jnp.concatenate([pj, qj], axis=0)
        mixed = jnp.dot(kr_ref[...], pq, preferred_element_type=F32) * scale
        f = jnp.dot(mixed.astype(BF16), w4_ref[...], preferred_element_type=F32)
        p_scr[:, pl.ds(w0, oct_w), :] = f.reshape(rows, oct_w, cg)
        return carry

    lax.fori_loop(0, width // oct_w, row_stage, 0)

    def gate_stage(i, carry):
        r0 = pl.multiple_of(i * tile, tile)
        g0 = pl.multiple_of(i * rows_per_tile, rows_per_tile)
        f = p_scr[pl.ds(g0, rows_per_tile), :, :].reshape(tile, cg)
        z = z_ref[pl.ds(r0, tile), :].astype(F32)
        o_ref[pl.ds(r0, tile), :] = (f * _silu(z)).astype(o_ref.dtype)
        return carry

    lax.fori_loop(0, t // tile, gate_stage, 0)


def _fourier_lat(proj, w4, groups, cg, u_blk, z_blk):
    bsz, t, _ = proj.shape
    width = GRID_W
    rows = t // width
    tile = 256
    oct_w = 8
    cc, sc = _dft_cos_sin(cg)
    cw, sw = _dft_cos_sin(width)
    cr, sr = _dft_cos_sin(rows)
    eye_t = np.eye(tile // width)
    bwc, bws = np.kron(eye_t, cw), np.kron(eye_t, sw)
    wp = np.concatenate([bwc, -bws], axis=1)
    wq = np.concatenate([bws, bwc], axis=1)
    eye_o = np.eye(oct_w)
    kr = np.concatenate([np.kron(cr, eye_o), -np.kron(sr, eye_o)], axis=1)
    consts = [jnp.asarray(m, F32).astype(BF16) for m in (cc, sc, wp, wq, kr)]
    body = functools.partial(_four_lat_body, scale=float(1.0 / np.sqrt(t * cg)), tile=tile, oct_w=oct_w)
    full = lambda a: pl.BlockSpec(a.shape, lambda b, g: (0, 0))
    return pl.pallas_call(
        body,
        grid=(bsz, groups),
        in_specs=[pl.BlockSpec((None, t, cg), lambda b, g: (b, 0, u_blk + g)),
                  pl.BlockSpec((None, t, cg), lambda b, g: (b, 0, z_blk + g))]
                 + [full(a) for a in consts]
                 + [pl.BlockSpec((None, cg, cg), lambda b, g: (g, 0, 0))],
        out_specs=pl.BlockSpec((None, t, cg), lambda b, g: (b, 0, g)),
        out_shape=jax.ShapeDtypeStruct((bsz, t, groups * cg), BF16),
        scratch_shapes=[pltpu.VMEM((rows, width, cg), F32), pltpu.VMEM((rows, width, cg), F32)],
        compiler_params=_params("arbitrary", "arbitrary"),
        name="fourier_lat",
    )(proj, proj, *consts, w4)


def _out_body(m_ref, f_ref, w_ref, x_ref, gate_ref, fw_ref, o_ref, *, kh, sub):
    k = pl.program_id(1)

    @pl.when(k == 0)
    def _():
        o_ref[...] = jnp.zeros_like(o_ref)

    @pl.when(k < kh)
    def _():
        o_ref[...] += jnp.dot(m_ref[...], w_ref[...], preferred_element_type=F32)

    @pl.when(k >= kh)
    def _():
        o_ref[...] += jnp.dot(f_ref[...], w_ref[...], preferred_element_type=F32)

    @pl.when(k == 2 * kh - 1)
    def _():
        gate = gate_ref[...]
        fw = fw_ref[...]

        def rows_step(i, carry):
            r0 = pl.multiple_of(i * sub, sub)
            y = x_ref[pl.ds(r0, sub), :] + gate * o_ref[pl.ds(r0, sub), :]
            ms = jnp.mean(y * y, axis=-1, keepdims=True)
            o_ref[pl.ds(r0, sub), :] = y * lax.rsqrt(ms + EPS) * fw
            return carry

        lax.fori_loop(0, o_ref.shape[0] // sub, rows_step, 0)


def _out_proj(m_out, f_out, w_out, x, mod, mod_row, final_w):
    bsz, t, d = x.shape
    mw = m_out.shape[-1]
    tm = min(512, t)
    tk = min(512, mw)
    kh = mw // tk
    tiles = t // tm
    m2 = m_out.reshape(bsz * t, mw)
    f2 = f_out.reshape(bsz * t, f_out.shape[-1])
    x2 = x.reshape(bsz * t, d)
    body = functools.partial(_out_body, kh=kh, sub=min(64, tm))
    y = pl.pallas_call(
        body,
        grid=(bsz * tiles, 2 * kh),
        in_specs=[pl.BlockSpec((tm, tk), lambda i, k: (i, jnp.minimum(k, kh - 1))),
                  pl.BlockSpec((tm, tk), lambda i, k: (i, jnp.maximum(k - kh, 0))),
                  pl.BlockSpec((tk, d), lambda i, k: (k, 0)),
                  pl.BlockSpec((tm, d), lambda i, k: (i, 0)),
                  pl.BlockSpec((None, 1, d), lambda i, k: (mod_row(i // tiles) * 3 + 2, 0, 0)),
                  pl.BlockSpec((1, d), lambda i, k: (0, 0))],
        out_specs=pl.BlockSpec((tm, d), lambda i, k: (i, 0)),
        out_shape=jax.ShapeDtypeStruct((bsz * t, d), F32),
        compiler_params=_params("arbitrary", "arbitrary"),
        name="out_proj",
    )(m2, f2, w_out, x2, mod, final_w)
    return y.reshape(bsz, t, d)


def _layer(x, mod, mod_row, norm_w, w_main, wg, wgt, bg, bgt, hnorm_w, w4, w_out, final_w,
           init, want_state, latent, heads, groups):
    bsz, t, d = x.shape
    mw = d // 2
    dv = mw // heads
    dqk = dv // 2
    cg = (d - mw) // groups
    h, gc, gr = _prenorm(x, mod, mod_row, norm_w, wg, wgt, bg, bgt, heads)
    proj = _in_proj(h.reshape(bsz * t, d), w_main).reshape(bsz, t, w_main.shape[1])
    res = _mlstm(proj, gc, gr, hnorm_w, init, heads, dqk, dv, want_state)
    u_blk = 4 * mw // cg
    z_blk = (4 * mw + (d - mw)) // cg
    four = _fourier_lat if latent else _fourier_ctx
    f_out = four(proj, w4, groups, cg, u_blk, z_blk)
    y = _out_proj(res[0], f_out, w_out, x, mod, mod_row, final_w)
    return y, res[1:]


def kernel(x_prompt, x_sample, c, state_C, state_n, state_m, c_ctx, w_ada, b_ada, norm_w, w_in, b_gates, hnorm_w, w_four, w_out, final_norm_w):
    depth = w_ada.shape[0]
    assert depth == 1, "single-layer step only"
    bp, _, d = x_prompt.shape
    bs = x_sample.shape[0]
    heads = state_C.shape[3]
    groups = w_four.shape[1]
    mw = d // 2
    n_gates = 4 * heads
    gate0 = 4 * mw
    assert 1 + bs <= MOD_ROWS

    cvec = jnp.zeros((MOD_ROWS, d), F32).at[0].set(c_ctx).at[1:1 + bs].set(c)
    mod = _modulation(cvec, w_ada[0], b_ada).reshape(MOD_ROWS * 3, 1, d)

    w_l = w_in[0]
    w_main = jnp.concatenate([w_l[:, :gate0], w_l[:, gate0 + n_gates:]], axis=1).astype(BF16)
    w_g = w_l[:, gate0:gate0 + n_gates].reshape(d, 4, heads).transpose(0, 2, 1)
    w_g = jnp.pad(w_g, ((0, 0), (0, 0), (0, GATE_SLOTS - 4))).reshape(d, heads * GATE_SLOTS)
    b_g = jnp.pad(b_gates[0].reshape(4, heads).T, ((0, 0), (0, GATE_SLOTS - 4))).reshape(1, heads * GATE_SLOTS)
    wg = w_g.astype(BF16)
    wgt = w_g.T.astype(BF16)
    bg = b_g.astype(F32)
    bgt = bg.T
    w4 = w_four[0].astype(BF16)
    wo = w_out[0].astype(BF16)
    shared = (norm_w, w_main, wg, wgt, bg, bgt, hnorm_w, w4, wo, final_norm_w.reshape(1, d))

    y_prompt, (c_new, n_new, m_new) = _layer(
        x_prompt, mod, lambda b: b * 0, *shared, None, True, False, heads, groups)
    init = (state_C[:, 0], state_n[:, 0][:, :, :, None, :], state_m[:, 0][:, :, :, None, None])
    y_sample, _ = _layer(
        x_sample, mod, lambda b: b + 1, *shared, init, False, True, heads, groups)

    new_c = c_new[:, None]
    new_n = n_new.reshape(bp, 1, 2, heads, -1)
    new_m = m_new.reshape(bp, 1, 2, heads)
    return (y_prompt, y_sample, new_c, new_n, new_m)
```

```python
import functools

import numpy as np
import jax
import jax.numpy as jnp
from jax import lax
from jax.experimental import pallas as pl
from jax.experimental.pallas import tpu as pltpu

F32 = jnp.float32
BF16 = jnp.bfloat16

EPS = 1e-6
GRID_W = 64
MLSTM_CHUNK = 256
GATE_SLOTS = 8
MOD_ROWS = 16
V7X_VMEM_BYTES = 64 * 1024 * 1024
VMEM_LIMIT = V7X_VMEM_BYTES - 4 * 1024 * 1024


def _params(*sem):
    return pltpu.CompilerParams(dimension_semantics=sem, vmem_limit_bytes=VMEM_LIMIT)


def _log_sigmoid(x):
    return jnp.minimum(x, 0.0) - jnp.log1p(jnp.exp(-jnp.abs(x)))


def _silu(x):
    return x * jax.nn.sigmoid(x)


def _mod_body(c_ref, w_ref, b_ref, o_ref):
    s = _silu(c_ref[...]).astype(BF16)
    o_ref[...] = jnp.dot(s, w_ref[...].astype(BF16), preferred_element_type=F32) + b_ref[...]


def _modulation(cvec, w_ada, b_ada):
    rows, d = cvec.shape
    n = w_ada.shape[1]
    tn = min(512, n)
    return pl.pallas_call(
        _mod_body,
        grid=(n // tn,),
        in_specs=[pl.BlockSpec((rows, d), lambda j: (0, 0)),
                  pl.BlockSpec((d, tn), lambda j: (0, j)),
                  pl.BlockSpec((1, tn), lambda j: (0, j))],
        out_specs=pl.BlockSpec((rows, tn), lambda j: (0, j)),
        out_shape=jax.ShapeDtypeStruct((rows, n), F32),
        compiler_params=_params("arbitrary"),
        name="modulation",
    )(cvec, w_ada, b_ada)


def _prenorm_body(x_ref, sh_ref, sc_ref, nw_ref, wg_ref, wgt_ref, bg_ref, bgt_ref,
                  h_ref, gc_ref, gr_ref, *, heads, sub):
    tm = x_ref.shape[0]
    nw = nw_ref[...]
    scale1 = 1.0 + sc_ref[...]
    shift = sh_ref[...]

    def rows_step(i, carry):
        r0 = pl.multiple_of(i * sub, sub)
        ms = jnp.mean(jnp.square(x_ref[pl.ds(r0, sub), :]), axis=-1, keepdims=True)
        y = x_ref[pl.ds(r0, sub), :] * lax.rsqrt(ms + EPS) * nw
        h_ref[pl.ds(r0, sub), :] = (y * scale1 + shift).astype(BF16)
        return carry

    lax.fori_loop(0, tm // sub, rows_step, 0)
    hb = h_ref[...]
    g = jnp.dot(hb, wg_ref[...], preferred_element_type=F32) + bg_ref[...]
    col = lax.broadcasted_iota(jnp.int32, g.shape, 1) % GATE_SLOTS
    g = jnp.where((col == 1) | (col == 3), _log_sigmoid(g), g)
    for hd in range(heads):
        gc_ref[hd] = g[:, hd * GATE_SLOTS:(hd + 1) * GATE_SLOTS]
    gt = lax.dot_general(wgt_ref[...], hb, (((1,), (1,)), ((), ())),
                         preferred_element_type=F32) + bgt_ref[...]
    row = lax.broadcasted_iota(jnp.int32, gt.shape, 0) % GATE_SLOTS
    gr_ref[...] = jnp.where((row == 1) | (row == 3), _log_sigmoid(gt), gt)


def _prenorm(x, mod, mod_row, norm_w, wg, wgt, bg, bgt, heads):
    bsz, t, d = x.shape
    tm = min(512, t)
    gw = heads * GATE_SLOTS
    body = functools.partial(_prenorm_body, heads=heads, sub=min(32, tm))
    return pl.pallas_call(
        body,
        grid=(bsz, t // tm),
        in_specs=[pl.BlockSpec((None, tm, d), lambda b, i: (b, i, 0)),
                  pl.BlockSpec((None, 1, d), lambda b, i: (mod_row(b) * 3, 0, 0)),
                  pl.BlockSpec((None, 1, d), lambda b, i: (mod_row(b) * 3 + 1, 0, 0)),
                  pl.BlockSpec((1, d), lambda b, i: (0, 0)),
                  pl.BlockSpec((d, gw), lambda b, i: (0, 0)),
                  pl.BlockSpec((gw, d), lambda b, i: (0, 0)),
                  pl.BlockSpec((1, gw), lambda b, i: (0, 0)),
                  pl.BlockSpec((gw, 1), lambda b, i: (0, 0))],
        out_specs=[pl.BlockSpec((None, tm, d), lambda b, i: (b, i, 0)),
                   pl.BlockSpec((None, heads, tm, GATE_SLOTS), lambda b, i: (b, 0, i, 0)),
                   pl.BlockSpec((None, gw, tm), lambda b, i: (b, 0, i))],
        out_shape=[jax.ShapeDtypeStruct((bsz, t, d), BF16),
                   jax.ShapeDtypeStruct((bsz, heads, t, GATE_SLOTS), F32),
                   jax.ShapeDtypeStruct((bsz, gw, t), F32)],
        compiler_params=_params("arbitrary", "arbitrary"),
        name="prenorm",
    )(x, mod, mod, norm_w, wg, wgt, bg, bgt)


def _in_proj_body(a_ref, wa_ref, wb_ref, oa_ref, ob_ref, *, na):
    j = pl.program_id(1)

    @pl.when(j < na)
    def _():
        oa_ref[...] = jnp.dot(a_ref[...], wa_ref[...], preferred_element_type=F32).astype(oa_ref.dtype)

    @pl.when(j >= na)
    def _():
        ob_ref[...] = jnp.dot(a_ref[...], wb_ref[...], preferred_element_type=F32).astype(ob_ref.dtype)


def _in_proj(h, wa, wb):
    m, k = h.shape
    tm = min(1024, m)
    tna = min(1024, wa.shape[1])
    tnb = min(512, wb.shape[1])
    na, nb = wa.shape[1] // tna, wb.shape[1] // tnb
    ja = lambda j: jnp.minimum(j, na - 1)
    jb = lambda j: jnp.maximum(j - na, 0)
    return pl.pallas_call(
        functools.partial(_in_proj_body, na=na),
        grid=(m // tm, na + nb),
        in_specs=[pl.BlockSpec((tm, k), lambda i, j: (i, 0)),
                  pl.BlockSpec((k, tna), lambda i, j: (0, ja(j))),
                  pl.BlockSpec((k, tnb), lambda i, j: (0, jb(j)))],
        out_specs=[pl.BlockSpec((tm, tna), lambda i, j: (i, ja(j))),
                   pl.BlockSpec((tm, tnb), lambda i, j: (i, jb(j)))],
        out_shape=[jax.ShapeDtypeStruct((m, wa.shape[1]), BF16),
                   jax.ShapeDtypeStruct((m, wb.shape[1]), BF16)],
        compiler_params=_params("arbitrary", "arbitrary"),
        name="in_proj",
    )(h, wa, wb)


def _mlstm_body(*refs, nc, has_init, has_state_out, k_scale):
    q_ref, k_ref, v_ref, o_ref, z_ref, gc_ref, gr_ref, hw_ref = refs[:8]
    pos = 8
    if has_init:
        c0_ref, n0_ref, m0_ref = refs[pos:pos + 3]
        pos += 3
    out_ref = refs[pos]
    pos += 1
    if has_state_out:
        cn_ref, nn_ref, mn_ref = refs[pos:pos + 3]
        pos += 3
    c_scr, n_scr, m_scr, hf_scr = refs[pos:]
    step = pl.program_id(2)
    chunk = q_ref.shape[0]

    def run(d):
        first = step == (0 if d == 0 else nc)
        last = step == (nc - 1 if d == 0 else 2 * nc - 1)
        ci = step if d == 0 else 2 * nc - 1 - step
        r0 = pl.multiple_of(ci * chunk, chunk)

        @pl.when(first)
        def _():
            if has_init:
                c_scr[...] = c0_ref[d]
                n_scr[...] = n0_ref[d]
                m_scr[...] = m0_ref[d]
            else:
                c_scr[...] = jnp.zeros_like(c_scr)
                n_scr[...] = jnp.zeros_like(n_scr)
                m_scr[...] = jnp.zeros_like(m_scr)

        q = q_ref[...]
        ks = k_ref[...] * jnp.asarray(k_scale, BF16)
        v = v_ref[...]
        g = gc_ref[...]
        gt = gr_ref[...]
        i_col, lf_col = g[:, 2 * d:2 * d + 1], g[:, 2 * d + 1:2 * d + 2]
        i_row, lf_row = gt[2 * d:2 * d + 1, :], gt[2 * d + 1:2 * d + 2, :]
        t_i = lax.broadcasted_iota(jnp.int32, (chunk, chunk), 0)
        s_i = lax.broadcasted_iota(jnp.int32, (chunk, chunk), 1)
        visible = (s_i <= t_i) if d == 0 else (s_i >= t_i)
        visible_t = (t_i <= s_i) if d == 0 else (t_i >= s_i)
        b_col = jnp.sum(jnp.where(visible, lf_row, 0.0), axis=1, keepdims=True)
        b_row = jnp.sum(jnp.where(visible_t, lf_col, 0.0), axis=0, keepdims=True)
        u_row = i_row - b_row
        u_col = i_col - b_col
        b_all = jnp.sum(lf_row, axis=1, keepdims=True)
        m_old = m_scr[...]
        c_old = c_scr[...]
        n_old = n_scr[...]

        logw = jnp.where(visible, b_col + u_row, -jnp.inf)
        a = b_col + m_old
        mt = jnp.maximum(a, jnp.max(logw, axis=1, keepdims=True))
        qk = lax.dot_general(q, ks, (((1,), (1,)), ((), ())), preferred_element_type=F32)
        sm = qk * jnp.exp(logw - mt)
        inter = jnp.exp(a - mt)
        num = (jnp.dot(sm.astype(BF16), v, preferred_element_type=F32)
               + inter * jnp.dot(q, c_old.astype(BF16), preferred_element_type=F32))
        qn = jnp.sum(q.astype(F32) * n_old, axis=1, keepdims=True)
        den = jnp.sum(sm, axis=1, keepdims=True) + inter * qn
        h = num / jnp.maximum(jnp.abs(den), jnp.exp(-mt))

        m_new = jnp.maximum(b_all + m_old, b_all + jnp.max(u_row, axis=1, keepdims=True))
        decay = jnp.exp(b_all + m_old - m_new)
        kw = ks.astype(F32) * jnp.exp(b_all + u_col - m_new)
        c_new = decay * c_old + lax.dot_general(kw.astype(BF16), v, (((0,), (0,)), ((), ())),
                                                preferred_element_type=F32)
        n_new = decay * n_old + jnp.sum(kw, axis=0, keepdims=True)
        c_scr[...] = c_new
        n_scr[...] = n_new
        m_scr[...] = m_new

        if has_state_out:
            @pl.when(last)
            def _():
                cn_ref[d] = c_new
                nn_ref[d] = n_new
                mn_ref[d] = m_new

        if d == 0:
            hf_scr[pl.ds(r0, chunk), :] = h
        else:
            hm = hf_scr[pl.ds(r0, chunk), :] + h
            hm = hm * lax.rsqrt(jnp.mean(hm * hm, axis=-1, keepdims=True) + EPS)
            gate = jax.nn.sigmoid(o_ref[...].astype(F32))
            out_ref[...] = (hm * hw_ref[...] * gate * _silu(z_ref[...].astype(F32))).astype(out_ref.dtype)

    @pl.when(step < nc)
    def _():
        run(0)

    @pl.when(step >= nc)
    def _():
        run(1)


def _mlstm(proj, gc, gr, hnorm_w, init, heads, dqk, dv, want_state):
    bsz, t, _ = proj.shape
    chunk = min(MLSTM_CHUNK, t)
    nc = t // chunk

    def ci(s):
        return jnp.where(s < nc, s, 2 * nc - 1 - s)

    def co(s):
        return jnp.where(s < nc, nc - 1, 2 * nc - 1 - s)

    in_specs = [
        pl.BlockSpec((None, chunk, dqk), lambda b, h, s: (b, ci(s), h)),
        pl.BlockSpec((None, chunk, dqk), lambda b, h, s: (b, ci(s), heads + h)),
        pl.BlockSpec((None, chunk, dv), lambda b, h, s: (b, ci(s), heads + h)),
        pl.BlockSpec((None, chunk, dv), lambda b, h, s: (b, co(s), 2 * heads + h)),
        pl.BlockSpec((None, chunk, dv), lambda b, h, s: (b, co(s), 3 * heads + h)),
        pl.BlockSpec((None, None, chunk, GATE_SLOTS), lambda b, h, s: (b, h, ci(s), 0)),
        pl.BlockSpec((None, GATE_SLOTS, chunk), lambda b, h, s: (b, h, ci(s))),
        pl.BlockSpec((1, dv), lambda b, h, s: (0, h)),
    ]
    args = [proj, proj, proj, proj, proj, gc, gr, hnorm_w]
    if init is not None:
        c0, n0, m0 = init
        in_specs += [
            pl.BlockSpec((None, 2, None, dqk, dv), lambda b, h, s: (b, 0, h, 0, 0)),
            pl.BlockSpec((None, 2, None, 1, dqk), lambda b, h, s: (b, 0, h, 0, 0)),
            pl.BlockSpec((None, 2, None, 1, 1), lambda b, h, s: (b, 0, h, 0, 0)),
        ]
        args += [c0, n0, m0]
    out_specs = [pl.BlockSpec((None, chunk, dv), lambda b, h, s: (b, co(s), h))]
    out_shape = [jax.ShapeDtypeStruct((bsz, t, heads * dv), BF16)]
    if want_state:
        out_specs += [
            pl.BlockSpec((None, 2, None, dqk, dv), lambda b, h, s: (b, 0, h, 0, 0)),
            pl.BlockSpec((None, 2, None, 1, dqk), lambda b, h, s: (b, 0, h, 0, 0)),
            pl.BlockSpec((None, 2, None, 1, 1), lambda b, h, s: (b, 0, h, 0, 0)),
        ]
        out_shape += [
            jax.ShapeDtypeStruct((bsz, 2, heads, dqk, dv), F32),
            jax.ShapeDtypeStruct((bsz, 2, heads, 1, dqk), F32),
            jax.ShapeDtypeStruct((bsz, 2, heads, 1, 1), F32),
        ]
    body = functools.partial(_mlstm_body, nc=nc, has_init=init is not None,
                             has_state_out=want_state, k_scale=dqk ** -0.5)
    return pl.pallas_call(
        body,
        grid=(bsz, heads, 2 * nc),
        in_specs=in_specs,
        out_specs=out_specs,
        out_shape=out_shape,
        scratch_shapes=[pltpu.VMEM((dqk, dv), F32), pltpu.VMEM((1, dqk), F32),
                        pltpu.VMEM((1, 1), F32), pltpu.VMEM((t, dv), F32)],
        compiler_params=_params("arbitrary", "arbitrary", "arbitrary"),
        name="mlstm",
    )(*args)


def _dft_cos_sin(n):
    ang = 2.0 * np.pi * np.outer(np.arange(n), np.arange(n)) / n
    return np.cos(ang), np.sin(ang)


def _four_ctx_body(u_ref, z_ref, cc_ref, sc_ref, ct_ref, st_ref, w4_ref, o_ref, *, scale):
    x = u_ref[...]
    a = jnp.dot(x, cc_ref[...], preferred_element_type=F32).astype(BF16)
    b = jnp.dot(x, sc_ref[...], preferred_element_type=F32).astype(BF16)
    mixed = (jnp.dot(ct_ref[...], a, preferred_element_type=F32)
             - jnp.dot(st_ref[...], b, preferred_element_type=F32)) * scale
    f = jnp.dot(mixed.astype(BF16), w4_ref[...], preferred_element_type=F32)
    o_ref[...] = (f * _silu(z_ref[...].astype(F32))).astype(o_ref.dtype)


def _fourier_ctx(proj, w4, groups, cg, u_blk, z_blk):
    bsz, t, _ = proj.shape
    cc, sc = _dft_cos_sin(cg)
    ct, st = _dft_cos_sin(t)
    consts = [jnp.asarray(m, F32).astype(BF16) for m in (cc, sc, ct, st)]
    body = functools.partial(_four_ctx_body, scale=float(1.0 / np.sqrt(t * cg)))
    sq = lambda n: pl.BlockSpec((n, n), lambda b, g: (0, 0))
    return pl.pallas_call(
        body,
        grid=(bsz, groups),
        in_specs=[pl.BlockSpec((None, t, cg), lambda b, g: (b, 0, u_blk + g)),
                  pl.BlockSpec((None, t, cg), lambda b, g: (b, 0, z_blk + g)),
                  sq(cg), sq(cg), sq(t), sq(t),
                  pl.BlockSpec((None, cg, cg), lambda b, g: (g, 0, 0))],
        out_specs=pl.BlockSpec((None, t, cg), lambda b, g: (b, 0, g)),
        out_shape=jax.ShapeDtypeStruct((bsz, t, groups * cg), BF16),
        compiler_params=_params("arbitrary", "arbitrary"),
        name="fourier_ctx",
    )(proj, proj, *consts, w4)


def _four_lat_body(u_ref, z_ref, cc_ref, sc_ref, wp_ref, wq_ref, kr_ref, w4_ref, o_ref,
                   p_scr, q_scr, *, scale, tile, oct_w):
    rows, width, cg = p_scr.shape
    t = rows * width
    rows_per_tile = tile // width

    def width_stage(i, carry):
        r0 = pl.multiple_of(i * tile, tile)
        x = u_ref[pl.ds(r0, tile), :]
        a = jnp.dot(x, cc_ref[...], preferred_element_type=F32).astype(BF16)
        b = jnp.dot(x, sc_ref[...], preferred_element_type=F32).astype(BF16)
        ab = jnp.concatenate([a, b], axis=0)
        p = jnp.dot(wp_ref[...], ab, preferred_element_type=F32)
        q = jnp.dot(wq_ref[...], ab, preferred_element_type=F32)
        g0 = pl.multiple_of(i * rows_per_tile, rows_per_tile)
        p_scr[pl.ds(g0, rows_per_tile), :, :] = p.reshape(rows_per_tile, width, cg)
        q_scr[pl.ds(g0, rows_per_tile), :, :] = q.reshape(rows_per_tile, width, cg)
        return carry

    lax.fori_loop(0, t // tile, width_stage, 0)

    def row_stage(j, carry):
        w0 = pl.multiple_of(j * oct_w, oct_w)
        pj = p_scr[:, pl.ds(w0, oct_w), :].reshape(rows * oct_w, cg).astype(BF16)
        qj = q_scr[:, pl.ds(w0, oct_w), :].reshape(rows * oct_w, cg).astype(BF16)
        pq = jnp.concatenate([pj, qj], axis=0)
        mixed = jnp.dot(kr_ref[...], pq, preferred_element_type=F32) * scale
        f = jnp.dot(mixed.astype(BF16), w4_ref[...], preferred_element_type=F32)
        p_scr[:, pl.ds(w0, oct_w), :] = f.reshape(rows, oct_w, cg)
        return carry

    lax.fori_loop(0, width // oct_w, row_stage, 0)

    def gate_stage(i, carry):
        r0 = pl.multiple_of(i * tile, tile)
        g0 = pl.multiple_of(i * rows_per_tile, rows_per_tile)
        f = p_scr[pl.ds(g0, rows_per_tile), :, :].reshape(tile, cg)
        z = z_ref[pl.ds(r0, tile), :].astype(F32)
        o_ref[pl.ds(r0, tile), :] = (f * _silu(z)).astype(o_ref.dtype)
        return carry

    lax.fori_loop(0, t // tile, gate_stage, 0)


def _fourier_lat(proj, w4, groups, cg, u_blk, z_blk):
    bsz, t, _ = proj.shape
    width = GRID_W
    rows = t // width
    tile = 256
    oct_w = 8
    cc, sc = _dft_cos_sin(cg)
    cw, sw = _dft_cos_sin(width)
    cr, sr = _dft_cos_sin(rows)
    eye_t = np.eye(tile // width)
    bwc, bws = np.kron(eye_t, cw), np.kron(eye_t, sw)
    wp = np.concatenate([bwc, -bws], axis=1)
    wq = np.concatenate([bws, bwc], axis=1)
    eye_o = np.eye(oct_w)
    kr = np.concatenate([np.kron(cr, eye_o), -np.kron(sr, eye_o)], axis=1)
    consts = [jnp.asarray(m, F32).astype(BF16) for m in (cc, sc, wp, wq, kr)]
    body = functools.partial(_four_lat_body, scale=float(1.0 / np.sqrt(t * cg)), tile=tile, oct_w=oct_w)
    full = lambda a: pl.BlockSpec(a.shape, lambda b, g: (0, 0))
    return pl.pallas_call(
        body,
        grid=(bsz, groups),
        in_specs=[pl.BlockSpec((None, t, cg), lambda b, g: (b, 0, u_blk + g)),
                  pl.BlockSpec((None, t, cg), lambda b, g: (b, 0, z_blk + g))]
                 + [full(a) for a in consts]
                 + [pl.BlockSpec((None, cg, cg), lambda b, g: (g, 0, 0))],
        out_specs=pl.BlockSpec((None, t, cg), lambda b, g: (b, 0, g)),
        out_shape=jax.ShapeDtypeStruct((bsz, t, groups * cg), BF16),
        scratch_shapes=[pltpu.VMEM((rows, width, cg), F32), pltpu.VMEM((rows, width, cg), F32)],
        compiler_params=_params("arbitrary", "arbitrary"),
        name="fourier_lat",
    )(proj, proj, *consts, w4)


def _out_body(m_ref, f_ref, w_ref, x_ref, gate_ref, fw_ref, o_ref, *, tn):
    mw = m_ref.shape[1]
    d = o_ref.shape[1]
    m = m_ref[...]
    f = f_ref[...]
    ssq = jnp.zeros((o_ref.shape[0], 1), F32)
    for n in range(d // tn):
        cols = slice(n * tn, (n + 1) * tn)
        acc = (jnp.dot(m, w_ref[:mw, cols], preferred_element_type=F32)
               + jnp.dot(f, w_ref[mw:, cols], preferred_element_type=F32))
        y = x_ref[:, cols] + gate_ref[:, cols] * acc
        ssq = ssq + jnp.sum(y * y, axis=-1, keepdims=True)
        o_ref[:, cols] = y
    inv = lax.rsqrt(ssq * (1.0 / d) + EPS)
    for n in range(d // tn):
        cols = slice(n * tn, (n + 1) * tn)
        o_ref[:, cols] = o_ref[:, cols] * inv * fw_ref[:, cols]


def _out_proj(m_out, f_out, w_out, x, mod, mod_row, final_w):
    bsz, t, d = x.shape
    mw = m_out.shape[-1]
    tm = min(256, t)
    tiles = t // tm
    m2 = m_out.reshape(bsz * t, mw)
    f2 = f_out.reshape(bsz * t, f_out.shape[-1])
    x2 = x.reshape(bsz * t, d)
    y = pl.pallas_call(
        functools.partial(_out_body, tn=min(512, d)),
        grid=(bsz * tiles,),
        in_specs=[pl.BlockSpec((tm, mw), lambda i: (i, 0)),
                  pl.BlockSpec((tm, f2.shape[1]), lambda i: (i, 0)),
                  pl.BlockSpec(w_out.shape, lambda i: (0, 0), pipeline_mode=pl.Buffered(1)),
                  pl.BlockSpec((tm, d), lambda i: (i, 0)),
                  pl.BlockSpec((None, 1, d), lambda i: (mod_row(i // tiles) * 3 + 2, 0, 0)),
                  pl.BlockSpec((1, d), lambda i: (0, 0))],
        out_specs=pl.BlockSpec((tm, d), lambda i: (i, 0)),
        out_shape=jax.ShapeDtypeStruct((bsz * t, d), F32),
        compiler_params=_params("arbitrary"),
        name="out_proj",
    )(m2, f2, w_out, x2, mod, final_w)
    return y.reshape(bsz, t, d)


def _layer(x, mod, mod_row, norm_w, w_m, w_f, wg, wgt, bg, bgt, hnorm_w, w4, w_out, final_w,
           init, want_state, latent, heads, groups):
    bsz, t, d = x.shape
    mw = d // 2
    dv = mw // heads
    dqk = dv // 2
    cg = (d - mw) // groups
    h, gc, gr = _prenorm(x, mod, mod_row, norm_w, wg, wgt, bg, bgt, heads)
    proj_m, proj_f = _in_proj(h.reshape(bsz * t, d), w_m, w_f)
    res = _mlstm(proj_m.reshape(bsz, t, -1), gc, gr, hnorm_w, init, heads, dqk, dv, want_state)
    four = _fourier_lat if latent else _fourier_ctx
    f_out = four(proj_f.reshape(bsz, t, -1), w4, groups, cg, 0, groups)
    y = _out_proj(res[0], f_out, w_out, x, mod, mod_row, final_w)
    return y, res[1:]


def kernel(x_prompt, x_sample, c, state_C, state_n, state_m, c_ctx, w_ada, b_ada, norm_w, w_in, b_gates, hnorm_w, w_four, w_out, final_norm_w):
    depth = w_ada.shape[0]
    assert depth == 1, "single-layer step only"
    bp, _, d = x_prompt.shape
    bs = x_sample.shape[0]
    heads = state_C.shape[3]
    groups = w_four.shape[1]
    mw = d // 2
    n_gates = 4 * heads
    gate0 = 4 * mw
    assert 1 + bs <= MOD_ROWS

    cvec = jnp.zeros((MOD_ROWS, d), F32).at[0].set(c_ctx).at[1:1 + bs].set(c)
    mod = _modulation(cvec, w_ada[0], b_ada).reshape(MOD_ROWS * 3, 1, d)

    w_l = w_in[0]
    w_m = w_l[:, :gate0].astype(BF16)
    w_f = w_l[:, gate0 + n_gates:].astype(BF16)
    w_g = w_l[:, gate0:gate0 + n_gates].reshape(d, 4, heads).transpose(0, 2, 1)
    w_g = jnp.pad(w_g, ((0, 0), (0, 0), (0, GATE_SLOTS - 4))).reshape(d, heads * GATE_SLOTS)
    b_g = jnp.pad(b_gates[0].reshape(4, heads).T, ((0, 0), (0, GATE_SLOTS - 4))).reshape(1, heads * GATE_SLOTS)
    wg = w_g.astype(BF16)
    wgt = w_g.T.astype(BF16)
    bg = b_g.astype(F32)
    bgt = bg.T
    w4 = w_four[0].astype(BF16)
    wo = w_out[0].astype(BF16)
    shared = (norm_w, w_m, w_f, wg, wgt, bg, bgt, hnorm_w, w4, wo, final_norm_w.reshape(1, d))

    y_prompt, (c_new, n_new, m_new) = _layer(
        x_prompt, mod, lambda b: b * 0, *shared, None, True, False, heads, groups)
    init = (state_C[:, 0], state_n[:, 0][:, :, :, None, :], state_m[:, 0][:, :, :, None, None])
    y_sample, _ = _layer(
        x_sample, mod, lambda b: b + 1, *shared, init, False, True, heads, groups)

    new_c = c_new[:, None]
    new_n = n_new.reshape(bp, 1, 2, heads, -1)
    new_m = m_new.reshape(bp, 1, 2, heads)
    return (y_prompt, y_sample, new_c, new_n, new_m)
```

```python
import functools

import numpy as np
import jax
import jax.numpy as jnp
from jax import lax
from jax.experimental import pallas as pl
from jax.experimental.pallas import tpu as pltpu

F32 = jnp.float32
BF16 = jnp.bfloat16

EPS = 1e-6
GRID_W = 64
MLSTM_CHUNK = 256
GATE_SLOTS = 8
LANES = 128
MOD_ROWS = 16
V7X_VMEM_BYTES = 64 * 1024 * 1024
VMEM_LIMIT = V7X_VMEM_BYTES - 4 * 1024 * 1024


def _params(*sem):
    return pltpu.CompilerParams(dimension_semantics=sem, vmem_limit_bytes=VMEM_LIMIT)


def _log_sigmoid(x):
    return jnp.minimum(x, 0.0) - jnp.log1p(jnp.exp(-jnp.abs(x)))


def _silu(x):
    return x * jax.nn.sigmoid(x)


def _mod_body(c_ref, w_ref, b_ref, o_ref):
    s = _silu(c_ref[...]).astype(BF16)
    o_ref[...] = jnp.dot(s, w_ref[...].astype(BF16), preferred_element_type=F32) + b_ref[...]


def _modulation(cvec, w_ada, b_ada):
    rows, d = cvec.shape
    n = w_ada.shape[1]
    tn = min(512, n)
    return pl.pallas_call(
        _mod_body,
        grid=(n // tn,),
        in_specs=[pl.BlockSpec((rows, d), lambda j: (0, 0)),
                  pl.BlockSpec((d, tn), lambda j: (0, j)),
                  pl.BlockSpec((1, tn), lambda j: (0, j))],
        out_specs=pl.BlockSpec((rows, tn), lambda j: (0, j)),
        out_shape=jax.ShapeDtypeStruct((rows, n), F32),
        compiler_params=_params("arbitrary"),
        name="modulation",
    )(cvec, w_ada, b_ada)


def _wprep_body(w_ref, o_ref, *, gate0, n_gates):
    o_ref[:, :gate0] = w_ref[:, :gate0].astype(BF16)
    o_ref[:, gate0:] = w_ref[:, gate0 + n_gates:].astype(BF16)


def _prep_w_in(w, gate0, n_gates):
    d, n = w.shape
    tr = min(128, d)
    return pl.pallas_call(
        functools.partial(_wprep_body, gate0=gate0, n_gates=n_gates),
        grid=(d // tr,),
        in_specs=[pl.BlockSpec((tr, n), lambda i: (i, 0))],
        out_specs=pl.BlockSpec((tr, n - n_gates), lambda i: (i, 0)),
        out_shape=jax.ShapeDtypeStruct((d, n - n_gates), BF16),
        compiler_params=_params("arbitrary"),
        name="w_in_cast",
    )(w)


def _split3(x):
    hi = x.astype(BF16)
    r1 = x - hi.astype(F32)
    mid = r1.astype(BF16)
    lo = (r1 - mid.astype(F32)).astype(BF16)
    return hi, mid, lo


def _prenorm_body(x_ref, sh_ref, sc_ref, nw_ref, wg_ref, bg_ref, low_ref, upp_ref,
                  h_ref, gc_ref, gr_ref, inv_scr, mul_scr, add_scr, *, heads, chunk, sub_ssq, sub_norm):
    tm = x_ref.shape[0]

    def ssq_step(i, carry):
        r0 = pl.multiple_of(i * sub_ssq, sub_ssq)
        ms = jnp.mean(jnp.square(x_ref[pl.ds(r0, sub_ssq), :]), axis=-1, keepdims=True)
        inv_scr[pl.ds(r0, sub_ssq), :] = lax.rsqrt(ms + EPS)
        return carry

    lax.fori_loop(0, tm // sub_ssq, ssq_step, 0)
    mul_scr[...] = jnp.broadcast_to(nw_ref[...] * (1.0 + sc_ref[...]), mul_scr.shape)
    add_scr[...] = jnp.broadcast_to(sh_ref[...], add_scr.shape)

    def norm_step(i, carry):
        r0 = pl.multiple_of(i * sub_norm, sub_norm)
        y = x_ref[pl.ds(r0, sub_norm), :] * inv_scr[pl.ds(r0, sub_norm), :]
        h_ref[pl.ds(r0, sub_norm), :] = (y * mul_scr[...] + add_scr[...]).astype(BF16)
        return carry

    lax.fori_loop(0, tm // sub_norm, norm_step, 0, unroll=2)

    g = jnp.dot(h_ref[...], wg_ref[...], preferred_element_type=F32) + bg_ref[...]
    slot = lax.broadcasted_iota(jnp.int32, (chunk, LANES), 1) % GATE_SLOTS
    for c in range(tm // chunk):
        gch = g[c * chunk:(c + 1) * chunk]
        lf = jnp.where((slot == 1) | (slot == 3), _log_sigmoid(gch), 0.0)
        hi, mid, lo = _split3(lf)
        lf3 = jnp.concatenate([hi, mid, lo], axis=1)
        pre = jnp.dot(low_ref[...], lf3, preferred_element_type=F32)
        suf = jnp.dot(upp_ref[...], lf3, preferred_element_type=F32)
        pre = pre[:, :LANES] + pre[:, LANES:2 * LANES] + pre[:, 2 * LANES:]
        suf = suf[:, :LANES] + suf[:, LANES:2 * LANES] + suf[:, 2 * LANES:]
        b = jnp.where(slot == 1, pre, jnp.where(slot == 3, suf, 0.0))
        u = gch - pltpu.roll(b, LANES - 1, 1)
        res = jnp.where((slot == 0) | (slot == 2), u, b)
        for hd in range(heads):
            gc_ref[hd, c * chunk:(c + 1) * chunk, :] = res[:, hd * GATE_SLOTS:(hd + 1) * GATE_SLOTS]
        gr_ref[:, c * chunk:(c + 1) * chunk] = res.T[:heads * GATE_SLOTS, :]


def _prenorm(x, mod, mod_row, norm_w, wg, bg, heads, chunk):
    bsz, t, d = x.shape
    tm = min(512, t)
    assert tm % chunk == 0
    gw = heads * GATE_SLOTS
    tri = np.tril(np.ones((chunk, chunk), np.float32))
    low, upp = jnp.asarray(tri).astype(BF16), jnp.asarray(tri.T).astype(BF16)
    body = functools.partial(_prenorm_body, heads=heads, chunk=chunk,
                             sub_ssq=min(64, tm), sub_norm=min(8, tm))
    return pl.pallas_call(
        body,
        grid=(bsz, t // tm),
        in_specs=[pl.BlockSpec((None, tm, d), lambda b, i: (b, i, 0)),
                  pl.BlockSpec((None, 1, d), lambda b, i: (mod_row(b) * 3, 0, 0)),
                  pl.BlockSpec((None, 1, d), lambda b, i: (mod_row(b) * 3 + 1, 0, 0)),
                  pl.BlockSpec((1, d), lambda b, i: (0, 0)),
                  pl.BlockSpec((d, LANES), lambda b, i: (0, 0)),
                  pl.BlockSpec((1, LANES), lambda b, i: (0, 0)),
                  pl.BlockSpec((chunk, chunk), lambda b, i: (0, 0)),
                  pl.BlockSpec((chunk, chunk), lambda b, i: (0, 0))],
        out_specs=[pl.BlockSpec((None, tm, d), lambda b, i: (b, i, 0)),
                   pl.BlockSpec((None, heads, tm, GATE_SLOTS), lambda b, i: (b, 0, i, 0)),
                   pl.BlockSpec((None, gw, tm), lambda b, i: (b, 0, i))],
        out_shape=[jax.ShapeDtypeStruct((bsz, t, d), BF16),
                   jax.ShapeDtypeStruct((bsz, heads, t, GATE_SLOTS), F32),
                   jax.ShapeDtypeStruct((bsz, gw, t), F32)],
        scratch_shapes=[pltpu.VMEM((tm, 1), F32), pltpu.VMEM((min(8, tm), d), F32),
                        pltpu.VMEM((min(8, tm), d), F32)],
        compiler_params=_params("arbitrary", "arbitrary"),
        name="prenorm",
    )(x, mod, mod, norm_w, wg, bg, low, upp)


def _matmul_body(a_ref, w_ref, o_ref):
    o_ref[...] = jnp.dot(a_ref[...], w_ref[...], preferred_element_type=F32).astype(o_ref.dtype)


def _in_proj(h, w):
    m, k = h.shape
    n = w.shape[1]
    tm, tn = min(1024, m), min(1024, n)
    return pl.pallas_call(
        _matmul_body,
        grid=(m // tm, n // tn),
        in_specs=[pl.BlockSpec((tm, k), lambda i, j: (i, 0)),
                  pl.BlockSpec((k, tn), lambda i, j: (0, j))],
        out_specs=pl.BlockSpec((tm, tn), lambda i, j: (i, j)),
        out_shape=jax.ShapeDtypeStruct((m, n), BF16),
        compiler_params=_params("arbitrary", "arbitrary"),
        name="in_proj",
    )(h, w)


def _mlstm_body(*refs, nc, has_init, has_state_out, k_scale):
    (qf_ref, kf_ref, vf_ref, gcf_ref, grf_ref, qb_ref, kb_ref, vb_ref, gcb_ref, grb_ref,
     o_ref, z_ref, hw_ref, lown_ref, uppn_ref) = refs[:15]
    pos = 15
    if has_init:
        c0_ref, n0_ref, m0_ref = refs[pos:pos + 3]
        pos += 3
    out_ref = refs[pos]
    pos += 1
    if has_state_out:
        cn_ref, nn_ref, mn_ref = refs[pos:pos + 3]
        pos += 3
    c_scr, m_scr, hf_scr, hb_scr = refs[pos:]
    step = pl.program_id(2)
    chunk, dqk = qf_ref.shape
    dv = vf_ref.shape[1]

    @pl.when(step == 0)
    def _():
        for d in range(2):
            if has_init:
                c_scr[d, :, :dv] = c0_ref[d]
                c_scr[d, :, dv:] = jnp.broadcast_to(n0_ref[d], (dqk, LANES))
                m_scr[d] = m0_ref[d]
            else:
                c_scr[d] = jnp.zeros(c_scr.shape[1:], F32)
                m_scr[d] = jnp.zeros(m_scr.shape[1:], F32)

    ones_tile = jnp.ones((chunk, LANES), BF16)

    def scan_chunk(d, q_ref, k_ref, v_ref, gc_ref, gr_ref):
        q = q_ref[...]
        ks = k_ref[...] * jnp.asarray(k_scale, BF16)
        v_aug = jnp.concatenate([v_ref[...], ones_tile], axis=1)
        u_col = gc_ref[:, 2 * d:2 * d + 1]
        b_col = gc_ref[:, 2 * d + 1:2 * d + 2]
        u_row = gr_ref[2 * d:2 * d + 1, :]
        b_all = b_col[chunk - 1:chunk] if d == 0 else b_col[0:1]
        m_old = m_scr[d]
        c_old = c_scr[d]

        um = u_row + (lown_ref if d == 0 else uppn_ref)[...]
        mt_u = jnp.maximum(m_old, jnp.max(um, axis=1, keepdims=True))
        qk = lax.dot_general(q, ks, (((1,), (1,)), ((), ())), preferred_element_type=F32)
        sm = (qk * jnp.exp(um - mt_u)).astype(BF16)
        q_in = q * jnp.exp(m_old - mt_u).astype(BF16)
        lhs = jnp.concatenate([sm, q_in], axis=1)
        rhs = jnp.concatenate([v_aug, c_old.astype(BF16)], axis=0)
        nd = jnp.dot(lhs, rhs, preferred_element_type=F32)
        den = nd[:, dv:dv + 1]
        h = nd[:, :dv] * (1.0 / jnp.maximum(jnp.abs(den), jnp.exp(-(b_col + mt_u))))

        m_new = b_all + jnp.maximum(m_old, jnp.max(u_row, axis=1, keepdims=True))
        decay = jnp.exp(b_all + m_old - m_new)
        kw = ks * jnp.exp(b_all + u_col - m_new).astype(BF16)
        c_scr[d] = decay * c_old + lax.dot_general(kw, v_aug, (((0,), (0,)), ((), ())),
                                                   preferred_element_type=F32)
        m_scr[d] = m_new
        return h

    cf = step
    cb = nc - 1 - step
    rf = pl.multiple_of(cf * chunk, chunk)
    rb = pl.multiple_of(cb * chunk, chunk)
    hf_scr[pl.ds(rf, chunk), :] = scan_chunk(0, qf_ref, kf_ref, vf_ref, gcf_ref, grf_ref)
    hb_scr[pl.ds(rb, chunk), :] = scan_chunk(1, qb_ref, kb_ref, vb_ref, gcb_ref, grb_ref)

    def finalize(r0):
        hm = hf_scr[pl.ds(r0, chunk), :] + hb_scr[pl.ds(r0, chunk), :]
        hm = hm * lax.rsqrt(jnp.mean(hm * hm, axis=-1, keepdims=True) + EPS)
        gate = jax.nn.sigmoid(o_ref[pl.ds(r0, chunk), :])
        zg = _silu(z_ref[pl.ds(r0, chunk), :])
        out_ref[pl.ds(r0, chunk), :] = hm.astype(BF16) * hw_ref[...].astype(BF16) * gate * zg

    if nc == 1:
        finalize(0)
    else:
        @pl.when(step >= nc // 2)
        def _():
            finalize(rf)
            finalize(rb)

    if has_state_out:
        @pl.when(step == nc - 1)
        def _():
            for d in range(2):
                cn_ref[d] = c_scr[d, :, :dv]
                nn_ref[d] = c_scr[d, :, dv:dv + 1]
                mn_ref[d] = m_scr[d]


def _mlstm(proj, gc, gr, hnorm_w, init, heads, dqk, dv, want_state):
    bsz, t, _ = proj.shape
    chunk = min(MLSTM_CHUNK, t)
    nc = t // chunk
    assert nc == 1 or nc % 2 == 0
    tri = np.tril(np.ones((chunk, chunk), bool))
    low = jnp.asarray(np.where(tri, 0.0, -np.inf).astype(np.float32))
    upp = jnp.asarray(np.where(tri.T, 0.0, -np.inf).astype(np.float32))

    def scan_specs(ci):
        return [
            pl.BlockSpec((None, chunk, dqk), lambda b, h, s: (b, ci(s), h)),
            pl.BlockSpec((None, chunk, dqk), lambda b, h, s: (b, ci(s), heads + h)),
            pl.BlockSpec((None, chunk, dv), lambda b, h, s: (b, ci(s), heads + h)),
            pl.BlockSpec((None, None, chunk, GATE_SLOTS), lambda b, h, s: (b, h, ci(s), 0)),
            pl.BlockSpec((None, GATE_SLOTS, chunk), lambda b, h, s: (b, h, ci(s))),
        ]

    in_specs = scan_specs(lambda s: s) + scan_specs(lambda s: nc - 1 - s) + [
        pl.BlockSpec((None, t, dv), lambda b, h, s: (b, 0, 2 * heads + h)),
        pl.BlockSpec((None, t, dv), lambda b, h, s: (b, 0, 3 * heads + h)),
        pl.BlockSpec((1, dv), lambda b, h, s: (0, h)),
        pl.BlockSpec((chunk, chunk), lambda b, h, s: (0, 0)),
        pl.BlockSpec((chunk, chunk), lambda b, h, s: (0, 0)),
    ]
    scan_args = [proj, proj, proj, gc, gr]
    args = scan_args + scan_args + [proj, proj, hnorm_w, low, upp]
    state_specs = [
        pl.BlockSpec((None, 2, None, dqk, dv), lambda b, h, s: (b, 0, h, 0, 0)),
        pl.BlockSpec((None, 2, None, dqk, 1), lambda b, h, s: (b, 0, h, 0, 0)),
        pl.BlockSpec((None, 2, None, 1, 1), lambda b, h, s: (b, 0, h, 0, 0)),
    ]
    if init is not None:
        in_specs += state_specs
        args += list(init)
    out_specs = [pl.BlockSpec((None, t, dv), lambda b, h, s: (b, 0, h))]
    out_shape = [jax.ShapeDtypeStruct((bsz, t, heads * dv), BF16)]
    if want_state:
        out_specs += state_specs
        out_shape += [
            jax.ShapeDtypeStruct((bsz, 2, heads, dqk, dv), F32),
            jax.ShapeDtypeStruct((bsz, 2, heads, dqk, 1), F32),
            jax.ShapeDtypeStruct((bsz, 2, heads, 1, 1), F32),
        ]
    body = functools.partial(_mlstm_body, nc=nc, has_init=init is not None,
                             has_state_out=want_state, k_scale=dqk ** -0.5)
    return pl.pallas_call(
        body,
        grid=(bsz, heads, nc),
        in_specs=in_specs,
        out_specs=out_specs,
        out_shape=out_shape,
        scratch_shapes=[pltpu.VMEM((2, dqk, dv + LANES), F32), pltpu.VMEM((2, 1, 1), F32),
                        pltpu.VMEM((t, dv), F32), pltpu.VMEM((t, dv), F32)],
        compiler_params=_params("arbitrary", "arbitrary", "arbitrary"),
        name="mlstm",
    )(*args)


def _dft_cos_sin(n):
    ang = 2.0 * np.pi * np.outer(np.arange(n), np.arange(n)) / n
    return np.cos(ang), np.sin(ang)


def _four_ctx_body(u_ref, z_ref, cc_ref, sc_ref, ct_ref, st_ref, w4_ref, o_ref, *, scale):
    x = u_ref[...]
    a = jnp.dot(x, cc_ref[...], preferred_element_type=F32).astype(BF16)
    b = jnp.dot(x, sc_ref[...], preferred_element_type=F32).astype(BF16)
    mixed = (jnp.dot(ct_ref[...], a, preferred_element_type=F32)
             - jnp.dot(st_ref[...], b, preferred_element_type=F32)) * scale
    f = jnp.dot(mixed.astype(BF16), w4_ref[...], preferred_element_type=F32)
    o_ref[...] = (f * _silu(z_ref[...].astype(F32))).astype(o_ref.dtype)


def _fourier_ctx(proj, w4, groups, cg, u_blk, z_blk):
    bsz, t, _ = proj.shape
    cc, sc = _dft_cos_sin(cg)
    ct, st = _dft_cos_sin(t)
    consts = [jnp.asarray(m, F32).astype(BF16) for m in (cc, sc, ct, st)]
    body = functools.partial(_four_ctx_body, scale=float(1.0 / np.sqrt(t * cg)))
    sq = lambda n: pl.BlockSpec((n, n), lambda b, g: (0, 0))
    return pl.pallas_call(
        body,
        grid=(bsz, groups),
        in_specs=[pl.BlockSpec((None, t, cg), lambda b, g: (b, 0, u_blk + g)),
                  pl.BlockSpec((None, t, cg), lambda b, g: (b, 0, z_blk + g)),
                  sq(cg), sq(cg), sq(t), sq(t),
                  pl.BlockSpec((None, cg, cg), lambda b, g: (g, 0, 0))],
        out_specs=pl.BlockSpec((None, t, cg), lambda b, g: (b, 0, g)),
        out_shape=jax.ShapeDtypeStruct((bsz, t, groups * cg), BF16),
        compiler_params=_params("arbitrary", "arbitrary"),
        name="fourier_ctx",
    )(proj, proj, *consts, w4)


def _four_lat_body(u_ref, z_ref, cc_ref, sc_ref, wpq_ref, kr_ref, w4_ref, o_ref,
                   p_scr, q_scr, *, scale, tile, oct_w):
    rows, width, cg = p_scr.shape
    t = rows * width
    rows_per_tile = tile // width

    def width_stage(i, carry):
        r0 = pl.multiple_of(i * tile, tile)
        x = u_ref[pl.ds(r0, tile), :]
        a = jnp.dot(x, cc_ref[...], preferred_element_type=F32).astype(BF16)
        b = jnp.dot(x, sc_ref[...], preferred_element_type=F32).astype(BF16)
        ab = jnp.concatenate([a, b], axis=0)
        pq = jnp.dot(wpq_ref[...], ab, preferred_element_type=F32)
        g0 = pl.multiple_of(i * rows_per_tile, rows_per_tile)
        p_scr[pl.ds(g0, rows_per_tile), :, :] = pq[:tile].reshape(rows_per_tile, width, cg)
        q_scr[pl.ds(g0, rows_per_tile), :, :] = pq[tile:].reshape(rows_per_tile, width, cg)
        return carry

    lax.fori_loop(0, t // tile, width_stage, 0, unroll=2)

    def row_stage(j, carry):
        w0 = pl.multiple_of(j * oct_w, oct_w)
        pj = p_scr[:, pl.ds(w0, oct_w), :].reshape(rows * oct_w, cg).astype(BF16)
        qj = q_scr[:, pl.ds(w0, oct_w), :].reshape(rows * oct_w, cg).astype(BF16)
        pq = jnp.concatenate([pj, qj], axis=0)
        mixed = jnp.dot(kr_ref[...], pq, preferred_element_type=F32) * scale
        f = jnp.dot(mixed.astype(BF16), w4_ref[...], preferred_element_type=F32)
        p_scr[:, pl.ds(w0, oct_w), :] = f.reshape(rows, oct_w, cg)
        return carry

    lax.fori_loop(0, width // oct_w, row_stage, 0, unroll=2)

    def gate_stage(i, carry):
        r0 = pl.multiple_of(i * tile, tile)
        g0 = pl.multiple_of(i * rows_per_tile, rows_per_tile)
        f = p_scr[pl.ds(g0, rows_per_tile), :, :].reshape(tile, cg)
        z = z_ref[pl.ds(r0, tile), :].astype(F32)
        o_ref[pl.ds(r0, tile), :] = (f * _silu(z)).astype(o_ref.dtype)
        return carry

    lax.fori_loop(0, t // tile, gate_stage, 0)


def _fourier_lat(proj, w4, groups, cg, u_blk, z_blk):
    bsz, t, _ = proj.shape
    width = GRID_W
    rows = t // width
    tile = 256
    oct_w = 8
    cc, sc = _dft_cos_sin(cg)
    cw, sw = _dft_cos_sin(width)
    cr, sr = _dft_cos_sin(rows)
    eye_t = np.eye(tile // width)
    bwc, bws = np.kron(eye_t, cw), np.kron(eye_t, sw)
    wpq = np.block([[bwc, -bws], [bws, bwc]])
    eye_o = np.eye(oct_w)
    kr = np.concatenate([np.kron(cr, eye_o), -np.kron(sr, eye_o)], axis=1)
    consts = [jnp.asarray(m, F32).astype(BF16) for m in (cc, sc, wpq, kr)]
    body = functools.partial(_four_lat_body, scale=float(1.0 / np.sqrt(t * cg)), tile=tile, oct_w=oct_w)
    full = lambda a: pl.BlockSpec(a.shape, lambda b, g: (0, 0))
    return pl.pallas_call(
        body,
        grid=(bsz, groups),
        in_specs=[pl.BlockSpec((None, t, cg), lambda b, g: (b, 0, u_blk + g)),
                  pl.BlockSpec((None, t, cg), lambda b, g: (b, 0, z_blk + g))]
                 + [full(a) for a in consts]
                 + [pl.BlockSpec((None, cg, cg), lambda b, g: (g, 0, 0))],
        out_specs=pl.BlockSpec((None, t, cg), lambda b, g: (b, 0, g)),
        out_shape=jax.ShapeDtypeStruct((bsz, t, groups * cg), BF16),
        scratch_shapes=[pltpu.VMEM((rows, width, cg), F32), pltpu.VMEM((rows, width, cg), F32)],
        compiler_params=_params("arbitrary", "arbitrary"),
        name="fourier_lat",
    )(proj, proj, *consts, w4)


def _out_body(m_ref, f_ref, w_ref, x_ref, gate_ref, fw_ref, o_ref, *, tn):
    mw = m_ref.shape[1]
    d = o_ref.shape[1]
    m = m_ref[...]
    f = f_ref[...]
    ssq = jnp.zeros((o_ref.shape[0], 1), F32)
    for n in range(d // tn):
        cols = slice(n * tn, (n + 1) * tn)
        acc = (jnp.dot(m, w_ref[:mw, cols], preferred_element_type=F32)
               + jnp.dot(f, w_ref[mw:, cols], preferred_element_type=F32))
        y = x_ref[:, cols] + gate_ref[:, cols] * acc
        ssq = ssq + jnp.sum(y * y, axis=-1, keepdims=True)
        o_ref[:, cols] = y
    inv = lax.rsqrt(ssq * (1.0 / d) + EPS)
    for n in range(d // tn):
        cols = slice(n * tn, (n + 1) * tn)
        o_ref[:, cols] = o_ref[:, cols] * inv * fw_ref[:, cols]


def _out_proj(m_out, f_out, w_out, x, mod, mod_row, final_w):
    bsz, t, d = x.shape
    mw = m_out.shape[-1]
    tm = min(256, t)
    tiles = t // tm
    m2 = m_out.reshape(bsz * t, mw)
    f2 = f_out.reshape(bsz * t, f_out.shape[-1])
    x2 = x.reshape(bsz * t, d)
    y = pl.pallas_call(
        functools.partial(_out_body, tn=min(512, d)),
        grid=(bsz * tiles,),
        in_specs=[pl.BlockSpec((tm, mw), lambda i: (i, 0)),
                  pl.BlockSpec((tm, f2.shape[1]), lambda i: (i, 0)),
                  pl.BlockSpec(w_out.shape, lambda i: (0, 0), pipeline_mode=pl.Buffered(1)),
                  pl.BlockSpec((tm, d), lambda i: (i, 0)),
                  pl.BlockSpec((None, 1, d), lambda i: (mod_row(i // tiles) * 3 + 2, 0, 0)),
                  pl.BlockSpec((1, d), lambda i: (0, 0))],
        out_specs=pl.BlockSpec((tm, d), lambda i: (i, 0)),
        out_shape=jax.ShapeDtypeStruct((bsz * t, d), F32),
        compiler_params=_params("arbitrary"),
        name="out_proj",
    )(m2, f2, w_out, x2, mod, final_w)
    return y.reshape(bsz, t, d)


def _layer(x, mod, mod_row, norm_w, w_main, wg, bg, hnorm_w, w4, w_out, final_w,
           init, want_state, latent, heads, groups):
    bsz, t, d = x.shape
    mw = d // 2
    dv = mw // heads
    dqk = dv // 2
    cg = (d - mw) // groups
    h, gc, gr = _prenorm(x, mod, mod_row, norm_w, wg, bg, heads, min(MLSTM_CHUNK, t))
    proj = _in_proj(h.reshape(bsz * t, d), w_main).reshape(bsz, t, w_main.shape[1])
    res = _mlstm(proj, gc, gr, hnorm_w, init, heads, dqk, dv, want_state)
    u_blk = 4 * mw // cg
    z_blk = (4 * mw + (d - mw)) // cg
    four = _fourier_lat if latent else _fourier_ctx
    f_out = four(proj, w4, groups, cg, u_blk, z_blk)
    y = _out_proj(res[0], f_out, w_out, x, mod, mod_row, final_w)
    return y, res[1:]


def kernel(x_prompt, x_sample, c, state_C, state_n, state_m, c_ctx, w_ada, b_ada, norm_w, w_in, b_gates, hnorm_w, w_four, w_out, final_norm_w):
    depth = w_ada.shape[0]
    assert depth == 1, "single-layer step only"
    bp, _, d = x_prompt.shape
    bs = x_sample.shape[0]
    heads = state_C.shape[3]
    groups = w_four.shape[1]
    mw = d // 2
    n_gates = 4 * heads
    gate0 = 4 * mw
    gw = heads * GATE_SLOTS
    assert 1 + bs <= MOD_ROWS and gw <= LANES

    cvec = jnp.zeros((MOD_ROWS, d), F32).at[0].set(c_ctx).at[1:1 + bs].set(c)
    mod = _modulation(cvec, w_ada[0], b_ada).reshape(MOD_ROWS * 3, 1, d)

    w_l = w_in[0]
    w_main = _prep_w_in(w_l, gate0, n_gates)
    w_g = w_l[:, gate0:gate0 + n_gates].reshape(d, 4, heads).transpose(0, 2, 1)
    w_g = jnp.pad(w_g, ((0, 0), (0, 0), (0, GATE_SLOTS - 4))).reshape(d, gw)
    wg = jnp.pad(w_g, ((0, 0), (0, LANES - gw))).astype(BF16)
    b_g = jnp.pad(b_gates[0].reshape(4, heads).T, ((0, 0), (0, GATE_SLOTS - 4))).reshape(1, gw)
    bg = jnp.pad(b_g, ((0, 0), (0, LANES - gw))).astype(F32)
    w4 = w_four[0].astype(BF16)
    wo = w_out[0].astype(BF16)
    shared = (norm_w, w_main, wg, bg, hnorm_w, w4, wo, final_norm_w.reshape(1, d))

    y_prompt, (c_new, n_new, m_new) = _layer(
        x_prompt, mod, lambda b: b * 0, *shared, None, True, False, heads, groups)
    init = (state_C[:, 0], state_n[:, 0][..., None], state_m[:, 0][..., None, None])
    y_sample, _ = _layer(
        x_sample, mod, lambda b: b + 1, *shared, init, False, True, heads, groups)

    new_c = c_new[:, None]
    new_n = n_new.reshape(bp, 1, 2, heads, -1)
    new_m = m_new.reshape(bp, 1, 2, heads)
    return (y_prompt, y_sample, new_c, new_n, new_m)
```

```python
import functools

import numpy as np
import jax
import jax.numpy as jnp
from jax import lax
from jax.experimental import pallas as pl
from jax.experimental.pallas import tpu as pltpu

F32 = jnp.float32
BF16 = jnp.bfloat16

EPS = 1e-6
GRID_W = 64
MLSTM_CHUNK = 256
GATE_SLOTS = 8
LANES = 128
MOD_ROWS = 16
V7X_VMEM_BYTES = 64 * 1024 * 1024
VMEM_LIMIT = V7X_VMEM_BYTES - 4 * 1024 * 1024


def _params(*sem):
    return pltpu.CompilerParams(dimension_semantics=sem, vmem_limit_bytes=VMEM_LIMIT)


def _log_sigmoid(x):
    return jnp.minimum(x, 0.0) - jnp.log1p(jnp.exp(-jnp.abs(x)))


def _silu(x):
    return x * jax.nn.sigmoid(x)


def _mod_body(c_ref, w_ref, b_ref, o_ref):
    s = _silu(c_ref[...]).astype(BF16)
    o_ref[...] = jnp.dot(s, w_ref[...].astype(BF16), preferred_element_type=F32) + b_ref[...]


def _modulation(cvec, w_ada, b_ada):
    rows, d = cvec.shape
    n = w_ada.shape[1]
    tn = min(512, n)
    return pl.pallas_call(
        _mod_body,
        grid=(n // tn,),
        in_specs=[pl.BlockSpec((rows, d), lambda j: (0, 0)),
                  pl.BlockSpec((d, tn), lambda j: (0, j)),
                  pl.BlockSpec((1, tn), lambda j: (0, j))],
        out_specs=pl.BlockSpec((rows, tn), lambda j: (0, j)),
        out_shape=jax.ShapeDtypeStruct((rows, n), F32),
        compiler_params=_params("arbitrary"),
        name="modulation",
    )(cvec, w_ada, b_ada)


def _wprep_body(w_ref, o_ref):
    o_ref[...] = w_ref[...].astype(BF16)


def _prep_w_in(w_t, gate0, n_gates, qk_rows):
    n, d = w_t.shape
    tr = 256
    assert qk_rows % tr == 0 and gate0 % tr == 0 and (n - gate0 - n_gates) % tr == 0 and n_gates % 8 == 0
    q_dst = n - n_gates - qk_rows

    def src_row(j):
        r = j * tr
        src = jnp.where(r < gate0 - qk_rows, r + qk_rows,
                        jnp.where(r < q_dst, r + qk_rows + n_gates, r - q_dst))
        return pl.multiple_of(src, 8)

    return pl.pallas_call(
        _wprep_body,
        grid=((n - n_gates) // tr,),
        in_specs=[pl.BlockSpec((pl.Element(tr), pl.Element(d)), lambda j: (src_row(j), 0))],
        out_specs=pl.BlockSpec((tr, d), lambda j: (j, 0)),
        out_shape=jax.ShapeDtypeStruct((n - n_gates, d), BF16),
        compiler_params=_params("arbitrary"),
        name="w_in_cast",
    )(w_t)


def _split3(x):
    hi = x.astype(BF16)
    r1 = x - hi.astype(F32)
    mid = r1.astype(BF16)
    lo = (r1 - mid.astype(F32)).astype(BF16)
    return hi, mid, lo


def _prenorm_body(x_ref, sh_ref, sc_ref, nw_ref, wg_ref, bg_ref, low_ref, upp_ref,
                  h_ref, gc_ref, gr_ref, inv_scr, mul_scr, add_scr, *, heads, chunk, sub_ssq, sub_norm):
    tm = x_ref.shape[0]

    def ssq_step(i, carry):
        r0 = pl.multiple_of(i * sub_ssq, sub_ssq)
        ms = jnp.mean(jnp.square(x_ref[pl.ds(r0, sub_ssq), :]), axis=-1, keepdims=True)
        inv_scr[pl.ds(r0, sub_ssq), :] = jnp.broadcast_to(lax.rsqrt(ms + EPS), (sub_ssq, LANES))
        return carry

    lax.fori_loop(0, tm // sub_ssq, ssq_step, 0)
    mul_scr[...] = jnp.broadcast_to(nw_ref[...] * (1.0 + sc_ref[...]), mul_scr.shape)
    add_scr[...] = jnp.broadcast_to(sh_ref[...], add_scr.shape)

    def norm_step(i, carry):
        r0 = pl.multiple_of(i * sub_norm, sub_norm)
        inv = inv_scr[pl.ds(r0, sub_norm), :]
        for lt in range(x_ref.shape[1] // LANES):
            cols = slice(lt * LANES, (lt + 1) * LANES)
            y = x_ref[pl.ds(r0, sub_norm), cols] * inv
            h_ref[pl.ds(r0, sub_norm), cols] = (y * mul_scr[:, cols] + add_scr[:, cols]).astype(BF16)
        return carry

    lax.fori_loop(0, tm // sub_norm, norm_step, 0, unroll=2)

    g = jnp.dot(h_ref[...], wg_ref[...], preferred_element_type=F32) + bg_ref[...]
    slot = lax.broadcasted_iota(jnp.int32, (chunk, LANES), 1) % GATE_SLOTS
    for c in range(tm // chunk):
        gch = g[c * chunk:(c + 1) * chunk]
        lf = jnp.where((slot == 1) | (slot == 3), _log_sigmoid(gch), 0.0)
        hi, mid, lo = _split3(lf)
        lf3 = jnp.concatenate([hi, mid, lo], axis=1)
        pre = jnp.dot(low_ref[...], lf3, preferred_element_type=F32)
        suf = jnp.dot(upp_ref[...], lf3, preferred_element_type=F32)
        pre = pre[:, :LANES] + pre[:, LANES:2 * LANES] + pre[:, 2 * LANES:]
        suf = suf[:, :LANES] + suf[:, LANES:2 * LANES] + suf[:, 2 * LANES:]
        b = jnp.where(slot == 1, pre, jnp.where(slot == 3, suf, 0.0))
        u = gch - pltpu.roll(b, LANES - 1, 1)
        res = jnp.where((slot == 0) | (slot == 2), u, b)
        for hd in range(heads):
            gc_ref[hd, c * chunk:(c + 1) * chunk, :] = res[:, hd * GATE_SLOTS:(hd + 1) * GATE_SLOTS]
        gr_ref[:, c * chunk:(c + 1) * chunk] = res.T[:heads * GATE_SLOTS, :]


def _prenorm(x, mod, mod_row, norm_w, wg, bg, heads, chunk):
    bsz, t, d = x.shape
    tm = min(512, t)
    assert tm % chunk == 0
    sub_norm = min(16, tm)
    gw = heads * GATE_SLOTS
    tri = np.tril(np.ones((chunk, chunk), np.float32))
    low, upp = jnp.asarray(tri).astype(BF16), jnp.asarray(tri.T).astype(BF16)
    body = functools.partial(_prenorm_body, heads=heads, chunk=chunk,
                             sub_ssq=min(64, tm), sub_norm=sub_norm)
    return pl.pallas_call(
        body,
        grid=(bsz, t // tm),
        in_specs=[pl.BlockSpec((None, tm, d), lambda b, i: (b, i, 0)),
                  pl.BlockSpec((None, 1, d), lambda b, i: (mod_row(b) * 3, 0, 0)),
                  pl.BlockSpec((None, 1, d), lambda b, i: (mod_row(b) * 3 + 1, 0, 0)),
                  pl.BlockSpec((1, d), lambda b, i: (0, 0)),
                  pl.BlockSpec((d, LANES), lambda b, i: (0, 0)),
                  pl.BlockSpec((1, LANES), lambda b, i: (0, 0)),
                  pl.BlockSpec((chunk, chunk), lambda b, i: (0, 0)),
                  pl.BlockSpec((chunk, chunk), lambda b, i: (0, 0))],
        out_specs=[pl.BlockSpec((None, tm, d), lambda b, i: (b, i, 0)),
                   pl.BlockSpec((None, heads, tm, GATE_SLOTS), lambda b, i: (b, 0, i, 0)),
                   pl.BlockSpec((None, gw, tm), lambda b, i: (b, 0, i))],
        out_shape=[jax.ShapeDtypeStruct((bsz, t, d), BF16),
                   jax.ShapeDtypeStruct((bsz, heads, t, GATE_SLOTS), F32),
                   jax.ShapeDtypeStruct((bsz, gw, t), F32)],
        scratch_shapes=[pltpu.VMEM((tm, LANES), F32), pltpu.VMEM((sub_norm, d), F32),
                        pltpu.VMEM((sub_norm, d), F32)],
        compiler_params=_params("arbitrary", "arbitrary"),
        name="prenorm",
    )(x, mod, mod, norm_w, wg, bg, low, upp)


def _in_proj_body(a_ref, w_ref, kt_ref, o_ref):
    j = pl.program_id(1)
    nt = (((1,), (1,)), ((), ()))

    @pl.when(j == 0)
    def _():
        kt_ref[...] = lax.dot_general(w_ref[...], a_ref[...], nt,
                                      preferred_element_type=F32).astype(kt_ref.dtype)

    @pl.when(j > 0)
    def _():
        o_ref[...] = lax.dot_general(a_ref[...], w_ref[...], nt,
                                     preferred_element_type=F32).astype(o_ref.dtype)


def _in_proj(h, w_t, k_rows):
    m, k = h.shape
    n = w_t.shape[0]
    tm, tn = min(1024, m), k_rows
    assert n % tn == 0 and tn % 256 == 0
    return pl.pallas_call(
        _in_proj_body,
        grid=(m // tm, n // tn),
        in_specs=[pl.BlockSpec((tm, k), lambda i, j: (i, 0)),
                  pl.BlockSpec((tn, k), lambda i, j: (j, 0))],
        out_specs=[pl.BlockSpec((tn, tm), lambda i, j: (0, i)),
                   pl.BlockSpec((tm, tn), lambda i, j: (i, jnp.maximum(j - 1, 0)))],
        out_shape=[jax.ShapeDtypeStruct((k_rows, m), BF16),
                   jax.ShapeDtypeStruct((m, n - k_rows), BF16)],
        compiler_params=_params("arbitrary", "arbitrary"),
        name="in_proj",
    )(h, w_t)


def _mlstm_body(*refs, nc, has_init, has_state_out, k_scale):
    (qf_ref, kf_ref, vf_ref, gcf_ref, grf_ref, qb_ref, kb_ref, vb_ref, gcb_ref, grb_ref,
     o_ref, z_ref, hw_ref, lown_ref, uppn_ref) = refs[:15]
    pos = 15
    if has_init:
        c0_ref, n0_ref, m0_ref = refs[pos:pos + 3]
        pos += 3
    out_ref = refs[pos]
    pos += 1
    if has_state_out:
        cn_ref, nn_ref, mn_ref = refs[pos:pos + 3]
        pos += 3
    c_scr, m_scr, hf_scr, hb_scr = refs[pos:]
    step = pl.program_id(2)
    chunk, dqk = qf_ref.shape
    dv = vf_ref.shape[1]

    @pl.when(step == 0)
    def _():
        for d in range(2):
            if has_init:
                c_scr[d, :, :dv] = c0_ref[d]
                c_scr[d, :, dv:] = jnp.broadcast_to(n0_ref[d], (dqk, LANES))
                m_scr[d] = m0_ref[d]
            else:
                c_scr[d] = jnp.zeros(c_scr.shape[1:], F32)
                m_scr[d] = jnp.zeros(m_scr.shape[1:], F32)

    ones_tile = jnp.ones((chunk, LANES), BF16)

    def scan_chunk(d, q_ref, k_ref, v_ref, gc_ref, gr_ref):
        q = q_ref[...]
        kt = k_ref[...] * jnp.asarray(k_scale, BF16)
        qk = jnp.dot(q, kt, preferred_element_type=F32)
        v_aug = jnp.concatenate([v_ref[...], ones_tile], axis=1)
        b_col = gc_ref[:, 2 * d + 1:2 * d + 2]
        u_row = gr_ref[2 * d:2 * d + 1, :]
        b_all = b_col[chunk - 1:chunk] if d == 0 else b_col[0:1]
        m_old = m_scr[d]
        c_old = c_scr[d]

        um = u_row + (lown_ref if d == 0 else uppn_ref)[...]
        mt_u = jnp.maximum(m_old, jnp.max(um, axis=1, keepdims=True))
        sm = (qk * jnp.exp(um - mt_u)).astype(BF16)
        q_in = q * jnp.exp(m_old - mt_u).astype(BF16)
        lhs = jnp.concatenate([sm, q_in], axis=1)
        rhs = jnp.concatenate([v_aug, c_old.astype(BF16)], axis=0)
        nd = jnp.dot(lhs, rhs, preferred_element_type=F32)
        den = nd[:, dv:dv + 1]
        h = nd[:, :dv] * (1.0 / jnp.maximum(jnp.abs(den), jnp.exp(-(b_col + mt_u))))

        m_new = b_all + jnp.maximum(m_old, jnp.max(u_row, axis=1, keepdims=True))
        decay = jnp.exp(b_all + m_old - m_new)
        kw_t = kt * jnp.exp(b_all + u_row - m_new).astype(BF16)
        c_scr[d] = decay * c_old + jnp.dot(kw_t, v_aug, preferred_element_type=F32)
        m_scr[d] = m_new
        return h

    cf = step
    cb = nc - 1 - step
    rf = pl.multiple_of(cf * chunk, chunk)
    rb = pl.multiple_of(cb * chunk, chunk)
    hf_scr[pl.ds(rf, chunk), :] = scan_chunk(0, qf_ref, kf_ref, vf_ref, gcf_ref, grf_ref)
    hb_scr[pl.ds(rb, chunk), :] = scan_chunk(1, qb_ref, kb_ref, vb_ref, gcb_ref, grb_ref)

    def finalize(r0):
        hm = hf_scr[pl.ds(r0, chunk), :] + hb_scr[pl.ds(r0, chunk), :]
        hm = hm * lax.rsqrt(jnp.mean(hm * hm, axis=-1, keepdims=True) + EPS)
        gate = jax.nn.sigmoid(o_ref[pl.ds(r0, chunk), :])
        zg = _silu(z_ref[pl.ds(r0, chunk), :])
        out_ref[pl.ds(r0, chunk), :] = hm.astype(BF16) * hw_ref[...].astype(BF16) * gate * zg

    if nc == 1:
        finalize(0)
    else:
        @pl.when(step >= nc // 2)
        def _():
            finalize(rf)
            finalize(rb)

    if has_state_out:
        @pl.when(step == nc - 1)
        def _():
            for d in range(2):
                cn_ref[d] = c_scr[d, :, :dv]
                nn_ref[d] = c_scr[d, :, dv:dv + 1]
                mn_ref[d] = m_scr[d]


def _mlstm(proj, k_t, gc, gr, hnorm_w, init, heads, dqk, dv, want_state):
    bsz, t, n_proj = proj.shape
    chunk = min(MLSTM_CHUNK, t)
    nc = t // chunk
    assert nc == 1 or nc % 2 == 0
    q_blk = (n_proj - heads * dqk) // dqk
    tri = np.tril(np.ones((chunk, chunk), bool))
    low = jnp.asarray(np.where(tri, 0.0, -np.inf).astype(np.float32))
    upp = jnp.asarray(np.where(tri.T, 0.0, -np.inf).astype(np.float32))

    def scan_specs(ci):
        return [
            pl.BlockSpec((None, chunk, dqk), lambda b, h, s: (b, ci(s), q_blk + h)),
            pl.BlockSpec((dqk, chunk), lambda b, h, s: (h, b * nc + ci(s))),
            pl.BlockSpec((None, chunk, dv), lambda b, h, s: (b, ci(s), h)),
            pl.BlockSpec((None, None, chunk, GATE_SLOTS), lambda b, h, s: (b, h, ci(s), 0)),
            pl.BlockSpec((None, GATE_SLOTS, chunk), lambda b, h, s: (b, h, ci(s))),
        ]

    in_specs = scan_specs(lambda s: s) + scan_specs(lambda s: nc - 1 - s) + [
        pl.BlockSpec((None, t, dv), lambda b, h, s: (b, 0, heads + h)),
        pl.BlockSpec((None, t, dv), lambda b, h, s: (b, 0, 2 * heads + h)),
        pl.BlockSpec((1, dv), lambda b, h, s: (0, h)),
        pl.BlockSpec((chunk, chunk), lambda b, h, s: (0, 0)),
        pl.BlockSpec((chunk, chunk), lambda b, h, s: (0, 0)),
    ]
    scan_args = [proj, k_t, proj, gc, gr]
    args = scan_args + scan_args + [proj, proj, hnorm_w, low, upp]
    state_specs = [
        pl.BlockSpec((None, 2, None, dqk, dv), lambda b, h, s: (b, 0, h, 0, 0)),
        pl.BlockSpec((None, 2, None, dqk, 1), lambda b, h, s: (b, 0, h, 0, 0)),
        pl.BlockSpec((None, 2, None, 1, 1), lambda b, h, s: (b, 0, h, 0, 0)),
    ]
    if init is not None:
        in_specs += state_specs
        args += list(init)
    out_specs = [pl.BlockSpec((None, t, dv), lambda b, h, s: (b, 0, h))]
    out_shape = [jax.ShapeDtypeStruct((bsz, t, heads * dv), BF16)]
    if want_state:
        out_specs += state_specs
        out_shape += [
            jax.ShapeDtypeStruct((bsz, 2, heads, dqk, dv), F32),
            jax.ShapeDtypeStruct((bsz, 2, heads, dqk, 1), F32),
            jax.ShapeDtypeStruct((bsz, 2, heads, 1, 1), F32),
        ]
    body = functools.partial(_mlstm_body, nc=nc, has_init=init is not None,
                             has_state_out=want_state, k_scale=dqk ** -0.5)
    return pl.pallas_call(
        body,
        grid=(bsz, heads, nc),
        in_specs=in_specs,
        out_specs=out_specs,
        out_shape=out_shape,
        scratch_shapes=[pltpu.VMEM((2, dqk, dv + LANES), F32), pltpu.VMEM((2, 1, 1), F32),
                        pltpu.VMEM((t, dv), F32), pltpu.VMEM((t, dv), F32)],
        compiler_params=_params("arbitrary", "arbitrary", "arbitrary"),
        name="mlstm",
    )(*args)


def _dft_cos_sin(n):
    ang = 2.0 * np.pi * np.outer(np.arange(n), np.arange(n)) / n
    return np.cos(ang), np.sin(ang)


def _four_ctx_body(u_ref, z_ref, cc_ref, sc_ref, ct_ref, st_ref, w4_ref, o_ref, *, scale):
    x = u_ref[...]
    a = jnp.dot(x, cc_ref[...], preferred_element_type=F32).astype(BF16)
    b = jnp.dot(x, sc_ref[...], preferred_element_type=F32).astype(BF16)
    mixed = (jnp.dot(ct_ref[...], a, preferred_element_type=F32)
             - jnp.dot(st_ref[...], b, preferred_element_type=F32)) * scale
    f = jnp.dot(mixed.astype(BF16), w4_ref[...], preferred_element_type=F32)
    o_ref[...] = (f * _silu(z_ref[...].astype(F32))).astype(o_ref.dtype)


def _fourier_ctx(proj, w4, groups, cg, u_blk, z_blk):
    bsz, t, _ = proj.shape
    cc, sc = _dft_cos_sin(cg)
    ct, st = _dft_cos_sin(t)
    consts = [jnp.asarray(m, F32).astype(BF16) for m in (cc, sc, ct, st)]
    body = functools.partial(_four_ctx_body, scale=float(1.0 / np.sqrt(t * cg)))
    sq = lambda n: pl.BlockSpec((n, n), lambda b, g: (0, 0))
    return pl.pallas_call(
        body,
        grid=(bsz, groups),
        in_specs=[pl.BlockSpec((None, t, cg), lambda b, g: (b, 0, u_blk + g)),
                  pl.BlockSpec((None, t, cg), lambda b, g: (b, 0, z_blk + g)),
                  sq(cg), sq(cg), sq(t), sq(t),
                  pl.BlockSpec((None, cg, cg), lambda b, g: (g, 0, 0))],
        out_specs=pl.BlockSpec((None, t, cg), lambda b, g: (b, 0, g)),
        out_shape=jax.ShapeDtypeStruct((bsz, t, groups * cg), BF16),
        compiler_params=_params("arbitrary", "arbitrary"),
        name="fourier_ctx",
    )(proj, proj, *consts, w4)


def _four_lat_body(u_ref, z_ref, cc_ref, sc_ref, wpq_ref, kr_ref, w4_ref, o_ref,
                   p_scr, q_scr, *, scale, tile, oct_w):
    rows, width, cg = p_scr.shape
    t = rows * width
    rows_per_tile = tile // width

    def width_stage(i, carry):
        r0 = pl.multiple_of(i * tile, tile)
        x = u_ref[pl.ds(r0, tile), :]
        a = jnp.dot(x, cc_ref[...], preferred_element_type=F32).astype(BF16)
        b = jnp.dot(x, sc_ref[...], preferred_element_type=F32).astype(BF16)
        ab = jnp.concatenate([a, b], axis=0)
        pq = jnp.dot(wpq_ref[...], ab, preferred_element_type=F32)
        g0 = pl.multiple_of(i * rows_per_tile, rows_per_tile)
        p_scr[pl.ds(g0, rows_per_tile), :, :] = pq[:tile].reshape(rows_per_tile, width, cg)
        q_scr[pl.ds(g0, rows_per_tile), :, :] = pq[tile:].reshape(rows_per_tile, width, cg)
        return carry

    lax.fori_loop(0, t // tile, width_stage, 0, unroll=2)

    def row_stage(j, carry):
        w0 = pl.multiple_of(j * oct_w, oct_w)
        pj = p_scr[:, pl.ds(w0, oct_w), :].reshape(rows * oct_w, cg).astype(BF16)
        qj = q_scr[:, pl.ds(w0, oct_w), :].reshape(rows * oct_w, cg).astype(BF16)
        pq = jnp.concatenate([pj, qj], axis=0)
        mixed = jnp.dot(kr_ref[...], pq, preferred_element_type=F32) * scale
        f = jnp.dot(mixed.astype(BF16), w4_ref[...], preferred_element_type=F32)
        p_scr[:, pl.ds(w0, oct_w), :] = f.reshape(rows, oct_w, cg)
        return carry

    lax.fori_loop(0, width // oct_w, row_stage, 0, unroll=2)

    def gate_stage(i, carry):
        r0 = pl.multiple_of(i * tile, tile)
        g0 = pl.multiple_of(i * rows_per_tile, rows_per_tile)
        f = p_scr[pl.ds(g0, rows_per_tile), :, :].reshape(tile, cg)
        z = z_ref[pl.ds(r0, tile), :].astype(F32)
        o_ref[pl.ds(r0, tile), :] = (f * _silu(z)).astype(o_ref.dtype)
        return carry

    lax.fori_loop(0, t // tile, gate_stage, 0)


def _fourier_lat(proj, w4, groups, cg, u_blk, z_blk):
    bsz, t, _ = proj.shape
    width = GRID_W
    rows = t // width
    tile = 256
    oct_w = 8
    cc, sc = _dft_cos_sin(cg)
    cw, sw = _dft_cos_sin(width)
    cr, sr = _dft_cos_sin(rows)
    eye_t = np.eye(tile // width)
    bwc, bws = np.kron(eye_t, cw), np.kron(eye_t, sw)
    wpq = np.block([[bwc, -bws], [bws, bwc]])
    eye_o = np.eye(oct_w)
    kr = np.concatenate([np.kron(cr, eye_o), -np.kron(sr, eye_o)], axis=1)
    consts = [jnp.asarray(m, F32).astype(BF16) for m in (cc, sc, wpq, kr)]
    body = functools.partial(_four_lat_body, scale=float(1.0 / np.sqrt(t * cg)), tile=tile, oct_w=oct_w)
    full = lambda a: pl.BlockSpec(a.shape, lambda b, g: (0, 0))
    return pl.pallas_call(
        body,
        grid=(bsz, groups),
        in_specs=[pl.BlockSpec((None, t, cg), lambda b, g: (b, 0, u_blk + g)),
                  pl.BlockSpec((None, t, cg), lambda b, g: (b, 0, z_blk + g))]
                 + [full(a) for a in consts]
                 + [pl.BlockSpec((None, cg, cg), lambda b, g: (g, 0, 0))],
        out_specs=pl.BlockSpec((None, t, cg), lambda b, g: (b, 0, g)),
        out_shape=jax.ShapeDtypeStruct((bsz, t, groups * cg), BF16),
        scratch_shapes=[pltpu.VMEM((rows, width, cg), F32), pltpu.VMEM((rows, width, cg), F32)],
        compiler_params=_params("arbitrary", "arbitrary"),
        name="fourier_lat",
    )(proj, proj, *consts, w4)


def _out_body(m_ref, f_ref, w_ref, x_ref, gate_ref, fw_ref, o_ref, *, tn):
    mw = m_ref.shape[1]
    d = o_ref.shape[1]
    m = m_ref[...]
    f = f_ref[...]
    ssq = jnp.zeros((o_ref.shape[0], 1), F32)
    for n in range(d // tn):
        cols = slice(n * tn, (n + 1) * tn)
        acc = (jnp.dot(m, w_ref[:mw, cols], preferred_element_type=F32)
               + jnp.dot(f, w_ref[mw:, cols], preferred_element_type=F32))
        y = x_ref[:, cols] + gate_ref[:, cols] * acc
        ssq = ssq + jnp.sum(y * y, axis=-1, keepdims=True)
        o_ref[:, cols] = y
    inv = lax.rsqrt(ssq * (1.0 / d) + EPS)
    for n in range(d // tn):
        cols = slice(n * tn, (n + 1) * tn)
        o_ref[:, cols] = o_ref[:, cols] * inv * fw_ref[:, cols]


def _out_proj(m_out, f_out, w_out, x, mod, mod_row, final_w):
    bsz, t, d = x.shape
    mw = m_out.shape[-1]
    tm = min(256, t)
    tiles = t // tm
    m2 = m_out.reshape(bsz * t, mw)
    f2 = f_out.reshape(bsz * t, f_out.shape[-1])
    x2 = x.reshape(bsz * t, d)
    y = pl.pallas_call(
        functools.partial(_out_body, tn=min(512, d)),
        grid=(bsz * tiles,),
        in_specs=[pl.BlockSpec((tm, mw), lambda i: (i, 0)),
                  pl.BlockSpec((tm, f2.shape[1]), lambda i: (i, 0)),
                  pl.BlockSpec(w_out.shape, lambda i: (0, 0), pipeline_mode=pl.Buffered(1)),
                  pl.BlockSpec((tm, d), lambda i: (i, 0)),
                  pl.BlockSpec((None, 1, d), lambda i: (mod_row(i // tiles) * 3 + 2, 0, 0)),
                  pl.BlockSpec((1, d), lambda i: (0, 0))],
        out_specs=pl.BlockSpec((tm, d), lambda i: (i, 0)),
        out_shape=jax.ShapeDtypeStruct((bsz * t, d), F32),
        compiler_params=_params("arbitrary"),
        name="out_proj",
    )(m2, f2, w_out, x2, mod, final_w)
    return y.reshape(bsz, t, d)


def _layer(x, mod, mod_row, norm_w, w_main, wg, bg, hnorm_w, w4, w_out, final_w,
           init, want_state, latent, heads, groups):
    bsz, t, d = x.shape
    mw = d // 2
    dv = mw // heads
    dqk = dv // 2
    cg = (d - mw) // groups
    h, gc, gr = _prenorm(x, mod, mod_row, norm_w, wg, bg, heads, min(MLSTM_CHUNK, t))
    k_t, proj = _in_proj(h.reshape(bsz * t, d), w_main, heads * dqk)
    proj = proj.reshape(bsz, t, -1)
    res = _mlstm(proj, k_t, gc, gr, hnorm_w, init, heads, dqk, dv, want_state)
    u_blk = 3 * mw // cg
    z_blk = (3 * mw + (d - mw)) // cg
    four = _fourier_lat if latent else _fourier_ctx
    f_out = four(proj, w4, groups, cg, u_blk, z_blk)
    y = _out_proj(res[0], f_out, w_out, x, mod, mod_row, final_w)
    return y, res[1:]


def kernel(x_prompt, x_sample, c, state_C, state_n, state_m, c_ctx, w_ada, b_ada, norm_w, w_in, b_gates, hnorm_w, w_four, w_out, final_norm_w):
    depth = w_ada.shape[0]
    assert depth == 1, "single-layer step only"
    bp, _, d = x_prompt.shape
    bs = x_sample.shape[0]
    heads = state_C.shape[3]
    groups = w_four.shape[1]
    mw = d // 2
    n_gates = 4 * heads
    gate0 = 4 * mw
    gw = heads * GATE_SLOTS
    assert 1 + bs <= MOD_ROWS and gw <= LANES

    cvec = jnp.zeros((MOD_ROWS, d), F32).at[0].set(c_ctx).at[1:1 + bs].set(c)
    mod = _modulation(cvec, w_ada[0], b_ada).reshape(MOD_ROWS * 3, 1, d)

    w_t = jnp.swapaxes(w_in, 1, 2)[0]
    w_main = _prep_w_in(w_t, gate0, n_gates, mw // 2)
    w_g = w_t[gate0:gate0 + n_gates].T.reshape(d, 4, heads).transpose(0, 2, 1)
    w_g = jnp.pad(w_g, ((0, 0), (0, 0), (0, GATE_SLOTS - 4))).reshape(d, gw)
    wg = jnp.pad(w_g, ((0, 0), (0, LANES - gw))).astype(BF16)
    b_g = jnp.pad(b_gates[0].reshape(4, heads).T, ((0, 0), (0, GATE_SLOTS - 4))).reshape(1, gw)
    bg = jnp.pad(b_g, ((0, 0), (0, LANES - gw))).astype(F32)
    w4 = w_four[0].astype(BF16)
    wo = w_out[0].astype(BF16)
    shared = (norm_w, w_main, wg, bg, hnorm_w, w4, wo, final_norm_w.reshape(1, d))

    y_prompt, (c_new, n_new, m_new) = _layer(
        x_prompt, mod, lambda b: b * 0, *shared, None, True, False, heads, groups)
    init = (state_C[:, 0], state_n[:, 0][..., None], state_m[:, 0][..., None, None])
    y_sample, _ = _layer(
        x_sample, mod, lambda b: b + 1, *shared, init, False, True, heads, groups)

    new_c = c_new[:, None]
    new_n = n_new.reshape(bp, 1, 2, heads, -1)
    new_m = m_new.reshape(bp, 1, 2, heads)
    return (y_prompt, y_sample, new_c, new_n, new_m)
```

```python
import functools

import numpy as np
import jax
import jax.numpy as jnp
from jax import lax
from jax.experimental import pallas as pl
from jax.experimental.pallas import tpu as pltpu

F32 = jnp.float32
BF16 = jnp.bfloat16

EPS = 1e-6
GRID_W = 64
MLSTM_CHUNK = 256
GATE_SLOTS = 8
LANES = 128
MOD_ROWS = 16
V7X_VMEM_BYTES = 64 * 1024 * 1024
VMEM_LIMIT = V7X_VMEM_BYTES - 4 * 1024 * 1024


def _params(*sem):
    return pltpu.CompilerParams(dimension_semantics=sem, vmem_limit_bytes=VMEM_LIMIT)


def _log_sigmoid(x):
    return jnp.minimum(x, 0.0) - jnp.log1p(jnp.exp(-jnp.abs(x)))


def _sigmoid(x):
    return 0.5 * jnp.tanh(0.5 * x) + 0.5


def _silu(x):
    return x * _sigmoid(x)


def _mod_body(c_ref, w_ref, b_ref, o_ref):
    s = _silu(c_ref[...]).astype(BF16)
    o_ref[...] = jnp.dot(s, w_ref[...].astype(BF16), preferred_element_type=F32) + b_ref[...]


def _modulation(cvec, w_ada, b_ada):
    rows, d = cvec.shape
    n = w_ada.shape[1]
    tn = min(512, n)
    return pl.pallas_call(
        _mod_body,
        grid=(n // tn,),
        in_specs=[pl.BlockSpec((rows, d), lambda j: (0, 0)),
                  pl.BlockSpec((d, tn), lambda j: (0, j)),
                  pl.BlockSpec((1, tn), lambda j: (0, j))],
        out_specs=pl.BlockSpec((rows, tn), lambda j: (0, j)),
        out_shape=jax.ShapeDtypeStruct((rows, n), F32),
        compiler_params=_params("arbitrary"),
        name="modulation",
    )(cvec, w_ada, b_ada)


def _wprep_body(w_ref, o_ref):
    o_ref[...] = w_ref[...].astype(BF16)


def _prep_w_in(w_t, gate0, n_gates, qk_rows):
    n, d = w_t.shape
    tr = 256
    assert qk_rows % tr == 0 and gate0 % tr == 0 and (n - gate0 - n_gates) % tr == 0 and n_gates % 8 == 0
    q_dst = n - n_gates - qk_rows

    def src_row(j):
        r = j * tr
        src = jnp.where(r < gate0 - qk_rows, r + qk_rows,
                        jnp.where(r < q_dst, r + qk_rows + n_gates, r - q_dst))
        return pl.multiple_of(src, 8)

    return pl.pallas_call(
        _wprep_body,
        grid=((n - n_gates) // tr,),
        in_specs=[pl.BlockSpec((pl.Element(tr), pl.Element(d)), lambda j: (src_row(j), 0))],
        out_specs=pl.BlockSpec((tr, d), lambda j: (j, 0)),
        out_shape=jax.ShapeDtypeStruct((n - n_gates, d), BF16),
        compiler_params=_params("arbitrary"),
        name="w_in_cast",
    )(w_t)


def _split3(x):
    hi = x.astype(BF16)
    r1 = x - hi.astype(F32)
    mid = r1.astype(BF16)
    lo = (r1 - mid.astype(F32)).astype(BF16)
    return hi, mid, lo


def _prenorm_body(x_ref, sh_ref, sc_ref, nw_ref, wg_ref, bg_ref, low_ref, upp_ref,
                  h_ref, gc_ref, gr_ref, inv_scr, mul_scr, add_scr, *, heads, chunk, sub_ssq, sub_norm):
    tm = x_ref.shape[0]

    def ssq_step(i, carry):
        r0 = pl.multiple_of(i * sub_ssq, sub_ssq)
        ms = jnp.mean(jnp.square(x_ref[pl.ds(r0, sub_ssq), :]), axis=-1, keepdims=True)
        inv_scr[pl.ds(r0, sub_ssq), :] = jnp.broadcast_to(lax.rsqrt(ms + EPS), (sub_ssq, LANES))
        return carry

    lax.fori_loop(0, tm // sub_ssq, ssq_step, 0)
    mul_scr[...] = jnp.broadcast_to(nw_ref[...] * (1.0 + sc_ref[...]), mul_scr.shape)
    add_scr[...] = jnp.broadcast_to(sh_ref[...], add_scr.shape)

    def norm_step(i, carry):
        r0 = pl.multiple_of(i * sub_norm, sub_norm)
        inv = inv_scr[pl.ds(r0, sub_norm), :]
        for lt in range(x_ref.shape[1] // LANES):
            cols = slice(lt * LANES, (lt + 1) * LANES)
            y = x_ref[pl.ds(r0, sub_norm), cols] * inv
            h_ref[pl.ds(r0, sub_norm), cols] = (y * mul_scr[:, cols] + add_scr[:, cols]).astype(BF16)
        return carry

    lax.fori_loop(0, tm // sub_norm, norm_step, 0, unroll=2)

    g = jnp.dot(h_ref[...], wg_ref[...], preferred_element_type=F32) + bg_ref[...]
    slot = lax.broadcasted_iota(jnp.int32, (chunk, LANES), 1) % GATE_SLOTS
    for c in range(tm // chunk):
        gch = g[c * chunk:(c + 1) * chunk]
        lf = jnp.where((slot == 1) | (slot == 3), _log_sigmoid(gch), 0.0)
        hi, mid, lo = _split3(lf)
        lf3 = jnp.concatenate([hi, mid, lo], axis=1)
        pre = jnp.dot(low_ref[...], lf3, preferred_element_type=F32)
        suf = jnp.dot(upp_ref[...], lf3, preferred_element_type=F32)
        pre = pre[:, :LANES] + pre[:, LANES:2 * LANES] + pre[:, 2 * LANES:]
        suf = suf[:, :LANES] + suf[:, LANES:2 * LANES] + suf[:, 2 * LANES:]
        b = jnp.where(slot == 1, pre, jnp.where(slot == 3, suf, 0.0))
        u = gch - pltpu.roll(b, LANES - 1, 1)
        res = jnp.where((slot == 0) | (slot == 2), u, b)
        for hd in range(heads):
            gc_ref[hd, c * chunk:(c + 1) * chunk, :] = res[:, hd * GATE_SLOTS:(hd + 1) * GATE_SLOTS]
        gr_ref[:, c * chunk:(c + 1) * chunk] = res.T[:heads * GATE_SLOTS, :]


def _prenorm(x, mod, mod_row, norm_w, wg, bg, heads, chunk):
    bsz, t, d = x.shape
    tm = min(512, t)
    assert tm % chunk == 0
    sub_norm = min(16, tm)
    gw = heads * GATE_SLOTS
    tri = np.tril(np.ones((chunk, chunk), np.float32))
    low, upp = jnp.asarray(tri).astype(BF16), jnp.asarray(tri.T).astype(BF16)
    body = functools.partial(_prenorm_body, heads=heads, chunk=chunk,
                             sub_ssq=min(64, tm), sub_norm=sub_norm)
    return pl.pallas_call(
        body,
        grid=(bsz, t // tm),
        in_specs=[pl.BlockSpec((None, tm, d), lambda b, i: (b, i, 0)),
                  pl.BlockSpec((None, 1, d), lambda b, i: (mod_row(b) * 3, 0, 0)),
                  pl.BlockSpec((None, 1, d), lambda b, i: (mod_row(b) * 3 + 1, 0, 0)),
                  pl.BlockSpec((1, d), lambda b, i: (0, 0)),
                  pl.BlockSpec((d, LANES), lambda b, i: (0, 0)),
                  pl.BlockSpec((1, LANES), lambda b, i: (0, 0)),
                  pl.BlockSpec((chunk, chunk), lambda b, i: (0, 0)),
                  pl.BlockSpec((chunk, chunk), lambda b, i: (0, 0))],
        out_specs=[pl.BlockSpec((None, tm, d), lambda b, i: (b, i, 0)),
                   pl.BlockSpec((None, heads, tm, GATE_SLOTS), lambda b, i: (b, 0, i, 0)),
                   pl.BlockSpec((None, gw, tm), lambda b, i: (b, 0, i))],
        out_shape=[jax.ShapeDtypeStruct((bsz, t, d), BF16),
                   jax.ShapeDtypeStruct((bsz, heads, t, GATE_SLOTS), F32),
                   jax.ShapeDtypeStruct((bsz, gw, t), F32)],
        scratch_shapes=[pltpu.VMEM((tm, LANES), F32), pltpu.VMEM((sub_norm, d), F32),
                        pltpu.VMEM((sub_norm, d), F32)],
        compiler_params=_params("arbitrary", "arbitrary"),
        name="prenorm",
    )(x, mod, mod, norm_w, wg, bg, low, upp)


def _in_proj_body(a_ref, w_ref, kt_ref, o_ref):
    j = pl.program_id(1)
    nt = (((1,), (1,)), ((), ()))

    @pl.when(j == 0)
    def _():
        kt_ref[...] = lax.dot_general(w_ref[...], a_ref[...], nt,
                                      preferred_element_type=F32).astype(kt_ref.dtype)

    @pl.when(j > 0)
    def _():
        o_ref[...] = lax.dot_general(a_ref[...], w_ref[...], nt,
                                     preferred_element_type=F32).astype(o_ref.dtype)


def _in_proj(h, w_t, k_rows):
    m, k = h.shape
    n = w_t.shape[0]
    tm, tn = min(1024, m), k_rows
    assert n % tn == 0 and tn % 256 == 0
    return pl.pallas_call(
        _in_proj_body,
        grid=(m // tm, n // tn),
        in_specs=[pl.BlockSpec((tm, k), lambda i, j: (i, 0)),
                  pl.BlockSpec((tn, k), lambda i, j: (j, 0))],
        out_specs=[pl.BlockSpec((tn, tm), lambda i, j: (0, i)),
                   pl.BlockSpec((tm, tn), lambda i, j: (i, jnp.maximum(j - 1, 0)))],
        out_shape=[jax.ShapeDtypeStruct((k_rows, m), BF16),
                   jax.ShapeDtypeStruct((m, n - k_rows), BF16)],
        compiler_params=_params("arbitrary", "arbitrary"),
        name="in_proj",
    )(h, w_t)


def _mlstm_body(*refs, nc, has_init, has_state_out, k_scale):
    (qf_ref, kf_ref, vf_ref, gcf_ref, grf_ref, qb_ref, kb_ref, vb_ref, gcb_ref, grb_ref,
     o_ref, z_ref, hw_ref, lown_ref, uppn_ref) = refs[:15]
    pos = 15
    if has_init:
        c0_ref, n0_ref, m0_ref = refs[pos:pos + 3]
        pos += 3
    out_ref = refs[pos]
    pos += 1
    if has_state_out:
        cn_ref, nn_ref, mn_ref = refs[pos:pos + 3]
        pos += 3
    c_scr, m_scr, hf_scr, hb_scr = refs[pos:]
    step = pl.program_id(2)
    chunk, dqk = qf_ref.shape
    dv = vf_ref.shape[1]

    @pl.when(step == 0)
    def _():
        for d in range(2):
            if has_init:
                c_scr[d, :, :dv] = c0_ref[d]
                c_scr[d, :, dv:] = jnp.broadcast_to(n0_ref[d], (dqk, LANES))
                m_scr[d] = m0_ref[d]
            else:
                c_scr[d] = jnp.zeros(c_scr.shape[1:], F32)
                m_scr[d] = jnp.zeros(m_scr.shape[1:], F32)

    ones_tile = jnp.ones((chunk, LANES), BF16)

    def scan_chunk(d, q_ref, k_ref, v_ref, gc_ref, gr_ref):
        q = q_ref[...]
        kt = k_ref[...] * jnp.asarray(k_scale, BF16)
        qk = jnp.dot(q, kt, preferred_element_type=F32)
        v_aug = jnp.concatenate([v_ref[...], ones_tile], axis=1)
        b_col = gc_ref[:, 2 * d + 1:2 * d + 2]
        u_row = gr_ref[2 * d:2 * d + 1, :]
        b_all = b_col[chunk - 1:chunk] if d == 0 else b_col[0:1]
        m_old = m_scr[d]
        c_old = c_scr[d]

        um = u_row + (lown_ref if d == 0 else uppn_ref)[...]
        mt_u = jnp.maximum(m_old, jnp.max(um, axis=1, keepdims=True))
        sm = (qk * jnp.exp(um - mt_u)).astype(BF16)
        q_in = q * jnp.exp(m_old - mt_u).astype(BF16)
        lhs = jnp.concatenate([sm, q_in], axis=1)
        rhs = jnp.concatenate([v_aug, c_old.astype(BF16)], axis=0)
        nd = jnp.dot(lhs, rhs, preferred_element_type=F32)
        den = nd[:, dv:dv + 1]
        h = nd[:, :dv] * (1.0 / jnp.maximum(jnp.abs(den), jnp.exp(-(b_col + mt_u))))

        m_new = b_all + jnp.maximum(m_old, jnp.max(u_row, axis=1, keepdims=True))
        decay = jnp.exp(b_all + m_old - m_new)
        kw_t = kt * jnp.exp(b_all + u_row - m_new).astype(BF16)
        c_scr[d] = decay * c_old + jnp.dot(kw_t, v_aug, preferred_element_type=F32)
        m_scr[d] = m_new
        return h

    cf = step
    cb = nc - 1 - step
    rf = pl.multiple_of(cf * chunk, chunk)
    rb = pl.multiple_of(cb * chunk, chunk)
    hf_scr[pl.ds(rf, chunk), :] = scan_chunk(0, qf_ref, kf_ref, vf_ref, gcf_ref, grf_ref)
    hb_scr[pl.ds(rb, chunk), :] = scan_chunk(1, qb_ref, kb_ref, vb_ref, gcb_ref, grb_ref)

    def finalize(r0):
        hm = hf_scr[pl.ds(r0, chunk), :] + hb_scr[pl.ds(r0, chunk), :]
        hm = hm * lax.rsqrt(jnp.mean(hm * hm, axis=-1, keepdims=True) + EPS)
        gate = _sigmoid(o_ref[pl.ds(r0, chunk), :])
        zg = _silu(z_ref[pl.ds(r0, chunk), :])
        out_ref[pl.ds(r0, chunk), :] = hm.astype(BF16) * hw_ref[...].astype(BF16) * gate * zg

    if nc == 1:
        finalize(0)
    else:
        @pl.when(step >= nc // 2)
        def _():
            finalize(rf)
            finalize(rb)

    if has_state_out:
        @pl.when(step == nc - 1)
        def _():
            for d in range(2):
                cn_ref[d] = c_scr[d, :, :dv]
                nn_ref[d] = c_scr[d, :, dv:dv + 1]
                mn_ref[d] = m_scr[d]


def _mlstm(proj, k_t, gc, gr, hnorm_w, init, heads, dqk, dv, want_state):
    bsz, t, n_proj = proj.shape
    chunk = min(MLSTM_CHUNK, t)
    nc = t // chunk
    assert nc == 1 or nc % 2 == 0
    q_blk = (n_proj - heads * dqk) // dqk
    tri = np.tril(np.ones((chunk, chunk), bool))
    low = jnp.asarray(np.where(tri, 0.0, -np.inf).astype(np.float32))
    upp = jnp.asarray(np.where(tri.T, 0.0, -np.inf).astype(np.float32))

    def scan_specs(ci):
        return [
            pl.BlockSpec((None, chunk, dqk), lambda b, h, s: (b, ci(s), q_blk + h)),
            pl.BlockSpec((dqk, chunk), lambda b, h, s: (h, b * nc + ci(s))),
            pl.BlockSpec((None, chunk, dv), lambda b, h, s: (b, ci(s), h)),
            pl.BlockSpec((None, None, chunk, GATE_SLOTS), lambda b, h, s: (b, h, ci(s), 0)),
            pl.BlockSpec((None, GATE_SLOTS, chunk), lambda b, h, s: (b, h, ci(s))),
        ]

    in_specs = scan_specs(lambda s: s) + scan_specs(lambda s: nc - 1 - s) + [
        pl.BlockSpec((None, t, dv), lambda b, h, s: (b, 0, heads + h)),
        pl.BlockSpec((None, t, dv), lambda b, h, s: (b, 0, 2 * heads + h)),
        pl.BlockSpec((1, dv), lambda b, h, s: (0, h)),
        pl.BlockSpec((chunk, chunk), lambda b, h, s: (0, 0)),
        pl.BlockSpec((chunk, chunk), lambda b, h, s: (0, 0)),
    ]
    scan_args = [proj, k_t, proj, gc, gr]
    args = scan_args + scan_args + [proj, proj, hnorm_w, low, upp]
    state_specs = [
        pl.BlockSpec((None, 2, None, dqk, dv), lambda b, h, s: (b, 0, h, 0, 0)),
        pl.BlockSpec((None, 2, None, dqk, 1), lambda b, h, s: (b, 0, h, 0, 0)),
        pl.BlockSpec((None, 2, None, 1, 1), lambda b, h, s: (b, 0, h, 0, 0)),
    ]
    if init is not None:
        in_specs += state_specs
        args += list(init)
    out_specs = [pl.BlockSpec((None, t, dv), lambda b, h, s: (b, 0, h))]
    out_shape = [jax.ShapeDtypeStruct((bsz, t, heads * dv), BF16)]
    if want_state:
        out_specs += state_specs
        out_shape += [
            jax.ShapeDtypeStruct((bsz, 2, heads, dqk, dv), F32),
            jax.ShapeDtypeStruct((bsz, 2, heads, dqk, 1), F32),
            jax.ShapeDtypeStruct((bsz, 2, heads, 1, 1), F32),
        ]
    body = functools.partial(_mlstm_body, nc=nc, has_init=init is not None,
                             has_state_out=want_state, k_scale=dqk ** -0.5)
    return pl.pallas_call(
        body,
        grid=(bsz, heads, nc),
        in_specs=in_specs,
        out_specs=out_specs,
        out_shape=out_shape,
        scratch_shapes=[pltpu.VMEM((2, dqk, dv + LANES), F32), pltpu.VMEM((2, 1, 1), F32),
                        pltpu.VMEM((t, dv), F32), pltpu.VMEM((t, dv), F32)],
        compiler_params=_params("arbitrary", "arbitrary", "arbitrary"),
        name="mlstm",
    )(*args)


def _dft_cos_sin(n):
    ang = 2.0 * np.pi * np.outer(np.arange(n), np.arange(n)) / n
    return np.cos(ang), np.sin(ang)


def _four_ctx_body(u_ref, z_ref, cc_ref, sc_ref, ct_ref, st_ref, w4_ref, o_ref, *, scale):
    cg = cc_ref.shape[0]
    for g in range(w4_ref.shape[0]):
        cols = slice(g * cg, (g + 1) * cg)
        x = u_ref[:, cols]
        a = jnp.dot(x, cc_ref[...], preferred_element_type=F32).astype(BF16)
        b = jnp.dot(x, sc_ref[...], preferred_element_type=F32).astype(BF16)
        mixed = (jnp.dot(ct_ref[...], a, preferred_element_type=F32)
                 - jnp.dot(st_ref[...], b, preferred_element_type=F32)) * scale
        f = jnp.dot(mixed.astype(BF16), w4_ref[g], preferred_element_type=F32)
        o_ref[:, cols] = (f * _silu(z_ref[:, cols].astype(F32))).astype(o_ref.dtype)


def _fourier_ctx(proj, w4, groups, cg, u_blk, z_blk):
    bsz, t, _ = proj.shape
    fw = groups * cg
    assert (u_blk * cg) % fw == 0 and (z_blk * cg) % fw == 0
    cc, sc = _dft_cos_sin(cg)
    ct, st = _dft_cos_sin(t)
    consts = [jnp.asarray(m, F32).astype(BF16) for m in (cc, sc, ct, st)]
    body = functools.partial(_four_ctx_body, scale=float(1.0 / np.sqrt(t * cg)))
    sq = lambda n: pl.BlockSpec((n, n), lambda b: (0, 0))
    return pl.pallas_call(
        body,
        grid=(bsz,),
        in_specs=[pl.BlockSpec((None, t, fw), lambda b: (b, 0, u_blk * cg // fw)),
                  pl.BlockSpec((None, t, fw), lambda b: (b, 0, z_blk * cg // fw)),
                  sq(cg), sq(cg), sq(t), sq(t),
                  pl.BlockSpec((groups, cg, cg), lambda b: (0, 0, 0))],
        out_specs=pl.BlockSpec((None, t, fw), lambda b: (b, 0, 0)),
        out_shape=jax.ShapeDtypeStruct((bsz, t, fw), BF16),
        compiler_params=_params("arbitrary"),
        name="fourier_ctx",
    )(proj, proj, *consts, w4)


def _four_lat_body(u_ref, z_ref, cc_ref, sc_ref, wpq_ref, kr_ref, w4_ref, o_ref,
                   p_scr, q_scr, f_scr, *, scale, tile, oct_w):
    rows, width, cg = p_scr.shape
    t = rows * width
    rows_per_tile = tile // width

    def width_stage(i, carry):
        r0 = pl.multiple_of(i * tile, tile)
        x = u_ref[pl.ds(r0, tile), :]
        a = jnp.dot(x, cc_ref[...], preferred_element_type=F32).astype(BF16)
        b = jnp.dot(x, sc_ref[...], preferred_element_type=F32).astype(BF16)
        ab = jnp.concatenate([a, b], axis=0)
        pq = jnp.dot(wpq_ref[...], ab, preferred_element_type=F32)
        g0 = pl.multiple_of(i * rows_per_tile, rows_per_tile)
        p_scr[pl.ds(g0, rows_per_tile), :, :] = pq[:tile].reshape(rows_per_tile, width, cg)
        q_scr[pl.ds(g0, rows_per_tile), :, :] = pq[tile:].reshape(rows_per_tile, width, cg)
        return carry

    lax.fori_loop(0, t // tile, width_stage, 0, unroll=4)

    def row_stage(j, carry):
        w0 = pl.multiple_of(j * oct_w, oct_w)
        pj = p_scr[:, pl.ds(w0, oct_w), :].reshape(rows * oct_w, cg).astype(BF16)
        qj = q_scr[:, pl.ds(w0, oct_w), :].reshape(rows * oct_w, cg).astype(BF16)
        pq = jnp.concatenate([pj, qj], axis=0)
        mixed = jnp.dot(kr_ref[...], pq, preferred_element_type=F32) * scale
        f = jnp.dot(mixed.astype(BF16), w4_ref[...], preferred_element_type=F32)
        f_scr[:, pl.ds(w0, oct_w), :] = f.reshape(rows, oct_w, cg)
        return carry

    lax.fori_loop(0, width // oct_w, row_stage, 0, unroll=4)

    def gate_stage(i, carry):
        r0 = pl.multiple_of(i * tile, tile)
        g0 = pl.multiple_of(i * rows_per_tile, rows_per_tile)
        f = f_scr[pl.ds(g0, rows_per_tile), :, :].reshape(tile, cg)
        z = z_ref[pl.ds(r0, tile), :].astype(F32)
        o_ref[pl.ds(r0, tile), :] = (f * _silu(z)).astype(o_ref.dtype)
        return carry

    lax.fori_loop(0, t // tile, gate_stage, 0)


def _fourier_lat(proj, w4, groups, cg, u_blk, z_blk):
    bsz, t, _ = proj.shape
    width = GRID_W
    rows = t // width
    tile = 256
    oct_w = 8
    cc, sc = _dft_cos_sin(cg)
    cw, sw = _dft_cos_sin(width)
    cr, sr = _dft_cos_sin(rows)
    eye_t = np.eye(tile // width)
    bwc, bws = np.kron(eye_t, cw), np.kron(eye_t, sw)
    wpq = np.block([[bwc, -bws], [bws, bwc]])
    eye_o = np.eye(oct_w)
    kr = np.concatenate([np.kron(cr, eye_o), -np.kron(sr, eye_o)], axis=1)
    consts = [jnp.asarray(m, F32).astype(BF16) for m in (cc, sc, wpq, kr)]
    body = functools.partial(_four_lat_body, scale=float(1.0 / np.sqrt(t * cg)), tile=tile, oct_w=oct_w)
    full = lambda a: pl.BlockSpec(a.shape, lambda b, g: (0, 0))
    return pl.pallas_call(
        body,
        grid=(bsz, groups),
        in_specs=[pl.BlockSpec((None, t, cg), lambda b, g: (b, 0, u_blk + g)),
                  pl.BlockSpec((None, t, cg), lambda b, g: (b, 0, z_blk + g))]
                 + [full(a) for a in consts]
                 + [pl.BlockSpec((None, cg, cg), lambda b, g: (g, 0, 0))],
        out_specs=pl.BlockSpec((None, t, cg), lambda b, g: (b, 0, g)),
        out_shape=jax.ShapeDtypeStruct((bsz, t, groups * cg), BF16),
        scratch_shapes=[pltpu.VMEM((rows, width, cg), F32)] * 3,
        compiler_params=_params("arbitrary", "arbitrary"),
        name="fourier_lat",
    )(proj, proj, *consts, w4)


def _out_body(m_ref, f_ref, w_ref, x_ref, gate_ref, fw_ref, o_ref, *, tn):
    mw = m_ref.shape[1]
    d = o_ref.shape[1]
    m = m_ref[...]
    f = f_ref[...]
    ssq = jnp.zeros((o_ref.shape[0], 1), F32)
    for n in range(d // tn):
        cols = slice(n * tn, (n + 1) * tn)
        acc = (jnp.dot(m, w_ref[:mw, cols], preferred_element_type=F32)
               + jnp.dot(f, w_ref[mw:, cols], preferred_element_type=F32))
        y = x_ref[:, cols] + gate_ref[:, cols] * acc
        ssq = ssq + jnp.sum(y * y, axis=-1, keepdims=True)
        o_ref[:, cols] = y
    inv = lax.rsqrt(ssq * (1.0 / d) + EPS)
    for n in range(d // tn):
        cols = slice(n * tn, (n + 1) * tn)
        o_ref[:, cols] = o_ref[:, cols] * inv * fw_ref[:, cols]


def _out_proj(m_out, f_out, w_out, x, mod, mod_row, final_w):
    bsz, t, d = x.shape
    mw = m_out.shape[-1]
    tm = min(256, t)
    tiles = t // tm
    m2 = m_out.reshape(bsz * t, mw)
    f2 = f_out.reshape(bsz * t, f_out.shape[-1])
    x2 = x.reshape(bsz * t, d)
    y = pl.pallas_call(
        functools.partial(_out_body, tn=min(512, d)),
        grid=(bsz * tiles,),
        in_specs=[pl.BlockSpec((tm, mw), lambda i: (i, 0)),
                  pl.BlockSpec((tm, f2.shape[1]), lambda i: (i, 0)),
                  pl.BlockSpec(w_out.shape, lambda i: (0, 0), pipeline_mode=pl.Buffered(1)),
                  pl.BlockSpec((tm, d), lambda i: (i, 0)),
                  pl.BlockSpec((None, 1, d), lambda i: (mod_row(i // tiles) * 3 + 2, 0, 0)),
                  pl.BlockSpec((1, d), lambda i: (0, 0))],
        out_specs=pl.BlockSpec((tm, d), lambda i: (i, 0)),
        out_shape=jax.ShapeDtypeStruct((bsz * t, d), F32),
        compiler_params=_params("arbitrary"),
        name="out_proj",
    )(m2, f2, w_out, x2, mod, final_w)
    return y.reshape(bsz, t, d)


def _layer(x, mod, mod_row, norm_w, w_main, wg, bg, hnorm_w, w4, w_out, final_w,
           init, want_state, latent, heads, groups):
    bsz, t, d = x.shape
    mw = d // 2
    dv = mw // heads
    dqk = dv // 2
    cg = (d - mw) // groups
    h, gc, gr = _prenorm(x, mod, mod_row, norm_w, wg, bg, heads, min(MLSTM_CHUNK, t))
    k_t, proj = _in_proj(h.reshape(bsz * t, d), w_main, heads * dqk)
    proj = proj.reshape(bsz, t, -1)
    res = _mlstm(proj, k_t, gc, gr, hnorm_w, init, heads, dqk, dv, want_state)
    u_blk = 3 * mw // cg
    z_blk = (3 * mw + (d - mw)) // cg
    four = _fourier_lat if latent else _fourier_ctx
    f_out = four(proj, w4, groups, cg, u_blk, z_blk)
    y = _out_proj(res[0], f_out, w_out, x, mod, mod_row, final_w)
    return y, res[1:]


def kernel(x_prompt, x_sample, c, state_C, state_n, state_m, c_ctx, w_ada, b_ada, norm_w, w_in, b_gates, hnorm_w, w_four, w_out, final_norm_w):
    depth = w_ada.shape[0]
    assert depth == 1, "single-layer step only"
    bp, _, d = x_prompt.shape
    bs = x_sample.shape[0]
    heads = state_C.shape[3]
    groups = w_four.shape[1]
    mw = d // 2
    n_gates = 4 * heads
    gate0 = 4 * mw
    gw = heads * GATE_SLOTS
    assert 1 + bs <= MOD_ROWS and gw <= LANES

    cvec = jnp.zeros((MOD_ROWS, d), F32).at[0].set(c_ctx).at[1:1 + bs].set(c)
    mod = _modulation(cvec, w_ada[0], b_ada).reshape(MOD_ROWS * 3, 1, d)

    w_t = jnp.swapaxes(w_in, 1, 2)[0]
    w_main = _prep_w_in(w_t, gate0, n_gates, mw // 2)
    w_g = w_t[gate0:gate0 + n_gates].T.reshape(d, 4, heads).transpose(0, 2, 1)
    w_g = jnp.pad(w_g, ((0, 0), (0, 0), (0, GATE_SLOTS - 4))).reshape(d, gw)
    wg = jnp.pad(w_g, ((0, 0), (0, LANES - gw))).astype(BF16)
    b_g = jnp.pad(b_gates[0].reshape(4, heads).T, ((0, 0), (0, GATE_SLOTS - 4))).reshape(1, gw)
    bg = jnp.pad(b_g, ((0, 0), (0, LANES - gw))).astype(F32)
    w4 = w_four[0].astype(BF16)
    wo = w_out[0].astype(BF16)
    shared = (norm_w, w_main, wg, bg, hnorm_w, w4, wo, final_norm_w.reshape(1, d))

    y_prompt, (c_new, n_new, m_new) = _layer(
        x_prompt, mod, lambda b: b * 0, *shared, None, True, False, heads, groups)
    init = (state_C[:, 0], state_n[:, 0][..., None], state_m[:, 0][..., None, None])
    y_sample, _ = _layer(
        x_sample, mod, lambda b: b + 1, *shared, init, False, True, heads, groups)

    new_c = c_new[:, None]
    new_n = n_new.reshape(bp, 1, 2, heads, -1)
    new_m = m_new.reshape(bp, 1, 2, heads)
    return (y_prompt, y_sample, new_c, new_n, new_m)
```

```python
import functools

import numpy as np
import jax
import jax.numpy as jnp
from jax import lax
from jax.experimental import pallas as pl
from jax.experimental.pallas import tpu as pltpu

F32 = jnp.float32
BF16 = jnp.bfloat16

EPS = 1e-6
GRID_W = 64
MLSTM_CHUNK = 256
GATE_SLOTS = 8
LANES = 128
MOD_ROWS = 16
V7X_VMEM_BYTES = 64 * 1024 * 1024
VMEM_LIMIT = V7X_VMEM_BYTES - 4 * 1024 * 1024


def _params(*sem):
    return pltpu.CompilerParams(dimension_semantics=sem, vmem_limit_bytes=VMEM_LIMIT)


def _log_sigmoid(x):
    return jnp.minimum(x, 0.0) - jnp.log1p(jnp.exp(-jnp.abs(x)))


def _sigmoid(x):
    return 0.5 * jnp.tanh(0.5 * x) + 0.5


def _silu(x):
    return x * _sigmoid(x)


def _mod_body(c_ref, w_ref, b_ref, o_ref):
    s = _silu(c_ref[...]).astype(BF16)
    o_ref[...] = jnp.dot(s, w_ref[...].astype(BF16), preferred_element_type=F32) + b_ref[...]


def _modulation(cvec, w_ada, b_ada):
    rows, d = cvec.shape
    n = w_ada.shape[1]
    tn = min(512, n)
    return pl.pallas_call(
        _mod_body,
        grid=(n // tn,),
        in_specs=[pl.BlockSpec((rows, d), lambda j: (0, 0)),
                  pl.BlockSpec((d, tn), lambda j: (0, j)),
                  pl.BlockSpec((1, tn), lambda j: (0, j))],
        out_specs=pl.BlockSpec((rows, tn), lambda j: (0, j)),
        out_shape=jax.ShapeDtypeStruct((rows, n), F32),
        compiler_params=_params("arbitrary"),
        name="modulation",
    )(cvec, w_ada, b_ada)


def _wprep_body(w_ref, o_ref):
    o_ref[...] = w_ref[...].astype(BF16)


def _prep_w_in(w_t, gate0, n_gates, qk_rows):
    n, d = w_t.shape
    tr = 256
    assert qk_rows % tr == 0 and gate0 % tr == 0 and (n - gate0 - n_gates) % tr == 0 and n_gates % 8 == 0
    q_dst = n - n_gates - qk_rows

    def src_row(j):
        r = j * tr
        src = jnp.where(r < gate0 - qk_rows, r + qk_rows,
                        jnp.where(r < q_dst, r + qk_rows + n_gates, r - q_dst))
        return pl.multiple_of(src, 8)

    return pl.pallas_call(
        _wprep_body,
        grid=((n - n_gates) // tr,),
        in_specs=[pl.BlockSpec((pl.Element(tr), pl.Element(d)), lambda j: (src_row(j), 0))],
        out_specs=pl.BlockSpec((tr, d), lambda j: (j, 0)),
        out_shape=jax.ShapeDtypeStruct((n - n_gates, d), BF16),
        compiler_params=_params("arbitrary"),
        name="w_in_cast",
    )(w_t)


def _split3(x):
    hi = x.astype(BF16)
    r1 = x - hi.astype(F32)
    mid = r1.astype(BF16)
    lo = (r1 - mid.astype(F32)).astype(BF16)
    return hi, mid, lo


def _norm_proj_body(x_ref, sh_ref, sc_ref, nw_ref, w_ref, wg_ref, bg_ref, low_ref, upp_ref,
                    kt_ref, o_ref, gc_ref, gr_ref, h_even, h_odd, mul_scr, add_scr,
                    *, n_tiles, n_sub, heads, chunk):
    i = pl.program_id(0)
    j = pl.program_id(1)
    sub, d = x_ref.shape
    nt = (((1,), (1,)), ((), ()))
    r0 = pl.multiple_of((j % n_sub) * sub, sub)

    def produce(h_dst):
        mul_scr[...] = jnp.broadcast_to(nw_ref[...] * (1.0 + sc_ref[...]), mul_scr.shape)
        add_scr[...] = jnp.broadcast_to(sh_ref[...], add_scr.shape)
        rows_per = mul_scr.shape[0]
        for g in range(sub // rows_per):
            rows = slice(g * rows_per, (g + 1) * rows_per)
            ms = jnp.mean(jnp.square(x_ref[rows, :]), axis=-1, keepdims=True)
            inv = jnp.broadcast_to(lax.rsqrt(ms + EPS), (rows_per, LANES))
            dst = pl.ds(r0 + g * rows_per, rows_per)
            for lt in range(d // LANES):
                cols = slice(lt * LANES, (lt + 1) * LANES)
                y = x_ref[rows, cols] * inv
                h_dst[dst, cols] = (y * mul_scr[:, cols] + add_scr[:, cols]).astype(BF16)

    def gates(h_src):
        g = jnp.dot(h_src[...], wg_ref[...], preferred_element_type=F32) + bg_ref[...]
        slot = lax.broadcasted_iota(jnp.int32, (chunk, LANES), 1) % GATE_SLOTS
        for c in range(h_src.shape[0] // chunk):
            gch = g[c * chunk:(c + 1) * chunk]
            lf = jnp.where((slot == 1) | (slot == 3), _log_sigmoid(gch), 0.0)
            hi, mid, lo = _split3(lf)
            lf3 = jnp.concatenate([hi, mid, lo], axis=1)
            pre = jnp.dot(low_ref[...], lf3, preferred_element_type=F32)
            suf = jnp.dot(upp_ref[...], lf3, preferred_element_type=F32)
            pre = pre[:, :LANES] + pre[:, LANES:2 * LANES] + pre[:, 2 * LANES:]
            suf = suf[:, :LANES] + suf[:, LANES:2 * LANES] + suf[:, 2 * LANES:]
            b = jnp.where(slot == 1, pre, jnp.where(slot == 3, suf, 0.0))
            u = gch - pltpu.roll(b, LANES - 1, 1)
            res = jnp.where((slot == 0) | (slot == 2), u, b)
            for hd in range(heads):
                gc_ref[hd, c * chunk:(c + 1) * chunk, :] = res[:, hd * GATE_SLOTS:(hd + 1) * GATE_SLOTS]
            gr_ref[:, c * chunk:(c + 1) * chunk] = res.T[:heads * GATE_SLOTS, :]

    def macro_step(h_dst, h_src):
        @pl.when(j == 0)
        def _():
            produce(h_dst)
            kt_ref[...] = lax.dot_general(w_ref[...], h_src[...], nt,
                                          preferred_element_type=F32).astype(kt_ref.dtype)

        @pl.when(j > 0)
        def _():
            produce(h_dst)
            o_ref[...] = lax.dot_general(h_src[...], w_ref[...], nt,
                                         preferred_element_type=F32).astype(o_ref.dtype)

    @pl.when(i == 0)
    def _():
        produce(h_even)

    @pl.when((i > 0) & (i % 2 == 1))
    def _():
        macro_step(h_odd, h_even)

    @pl.when((i > 0) & (i % 2 == 0))
    def _():
        macro_step(h_even, h_odd)

    last_sub = (j == n_sub - 1) & (i < n_tiles)

    @pl.when(last_sub & (i % 2 == 0))
    def _():
        gates(h_even)

    @pl.when(last_sub & (i % 2 == 1))
    def _():
        gates(h_odd)


def _norm_proj(x, mod, tile_mod_row, norm_w, w_t, k_rows, wg, bg, heads, chunk):
    m, d = x.shape
    n = w_t.shape[0]
    tm, tn = min(1024, m), k_rows
    sub = min(128, tm)
    n_tiles, n_sub, n_cols = m // tm, tm // sub, n // tn
    assert n % tn == 0 and tn % 256 == 0 and n_cols >= n_sub and tm % chunk == 0
    gw = heads * GATE_SLOTS
    rows_per = min(16, sub)
    tri = np.tril(np.ones((chunk, chunk), np.float32))
    low, upp = jnp.asarray(tri).astype(BF16), jnp.asarray(tri.T).astype(BF16)
    prod = lambda i: jnp.minimum(i, n_tiles - 1)
    cons = lambda i: jnp.maximum(i - 1, 0)
    ocol = lambda i, j: jnp.where(i == 0, 0, jnp.maximum(j - 1, 0))
    body = functools.partial(_norm_proj_body, n_tiles=n_tiles, n_sub=n_sub, heads=heads, chunk=chunk)
    return pl.pallas_call(
        body,
        grid=(n_tiles + 1, n_cols),
        in_specs=[pl.BlockSpec((sub, d), lambda i, j: (prod(i) * n_sub + j % n_sub, 0)),
                  pl.BlockSpec((None, 1, d), lambda i, j: (tile_mod_row(prod(i)) * 3, 0, 0)),
                  pl.BlockSpec((None, 1, d), lambda i, j: (tile_mod_row(prod(i)) * 3 + 1, 0, 0)),
                  pl.BlockSpec((1, d), lambda i, j: (0, 0)),
                  pl.BlockSpec((tn, d), lambda i, j: (j, 0)),
                  pl.BlockSpec((d, LANES), lambda i, j: (0, 0)),
                  pl.BlockSpec((1, LANES), lambda i, j: (0, 0)),
                  pl.BlockSpec((chunk, chunk), lambda i, j: (0, 0)),
                  pl.BlockSpec((chunk, chunk), lambda i, j: (0, 0))],
        out_specs=[pl.BlockSpec((tn, tm), lambda i, j: (0, cons(i))),
                   pl.BlockSpec((tm, tn), lambda i, j: (cons(i), ocol(i, j))),
                   pl.BlockSpec((heads, tm, GATE_SLOTS), lambda i, j: (0, prod(i), 0)),
                   pl.BlockSpec((gw, tm), lambda i, j: (0, prod(i)))],
        out_shape=[jax.ShapeDtypeStruct((k_rows, m), BF16),
                   jax.ShapeDtypeStruct((m, n - k_rows), BF16),
                   jax.ShapeDtypeStruct((heads, m, GATE_SLOTS), F32),
                   jax.ShapeDtypeStruct((gw, m), F32)],
        scratch_shapes=[pltpu.VMEM((tm, d), BF16), pltpu.VMEM((tm, d), BF16),
                        pltpu.VMEM((rows_per, d), F32), pltpu.VMEM((rows_per, d), F32)],
        compiler_params=_params("arbitrary", "arbitrary"),
        name="norm_proj",
    )(x, mod, mod, norm_w, w_t, wg, bg, low, upp)


def _mlstm_body(*refs, nc, has_init, has_state_out, k_scale):
    (qf_ref, kf_ref, vf_ref, gcf_ref, grf_ref, qb_ref, kb_ref, vb_ref, gcb_ref, grb_ref,
     o_ref, z_ref, hw_ref, lown_ref, uppn_ref) = refs[:15]
    pos = 15
    if has_init:
        c0_ref, n0_ref, m0_ref = refs[pos:pos + 3]
        pos += 3
    out_ref = refs[pos]
    pos += 1
    if has_state_out:
        cn_ref, nn_ref, mn_ref = refs[pos:pos + 3]
        pos += 3
    c_scr, m_scr, hf_scr, hb_scr = refs[pos:]
    step = pl.program_id(2)
    chunk, dqk = qf_ref.shape
    dv = vf_ref.shape[1]

    @pl.when(step == 0)
    def _():
        for d in range(2):
            if has_init:
                c_scr[d, :, :dv] = c0_ref[d]
                c_scr[d, :, dv:] = jnp.broadcast_to(n0_ref[d], (dqk, LANES))
                m_scr[d] = m0_ref[d]
            else:
                c_scr[d] = jnp.zeros(c_scr.shape[1:], F32)
                m_scr[d] = jnp.zeros(m_scr.shape[1:], F32)

    ones_tile = jnp.ones((chunk, LANES), BF16)

    def scan_chunk(d, q_ref, k_ref, v_ref, gc_ref, gr_ref):
        q = q_ref[...]
        kt = k_ref[...] * jnp.asarray(k_scale, BF16)
        qk = jnp.dot(q, kt, preferred_element_type=F32)
        v_aug = jnp.concatenate([v_ref[...], ones_tile], axis=1)
        b_col = gc_ref[:, 2 * d + 1:2 * d + 2]
        u_row = gr_ref[2 * d:2 * d + 1, :]
        b_all = b_col[chunk - 1:chunk] if d == 0 else b_col[0:1]
        m_old = m_scr[d]
        c_old = c_scr[d]

        um = u_row + (lown_ref if d == 0 else uppn_ref)[...]
        mt_u = jnp.maximum(m_old, jnp.max(um, axis=1, keepdims=True))
        sm = (qk * jnp.exp(um - mt_u)).astype(BF16)
        q_in = q * jnp.exp(m_old - mt_u).astype(BF16)
        lhs = jnp.concatenate([sm, q_in], axis=1)
        rhs = jnp.concatenate([v_aug, c_old.astype(BF16)], axis=0)
        nd = jnp.dot(lhs, rhs, preferred_element_type=F32)
        den = nd[:, dv:dv + 1]
        h = nd[:, :dv] * (1.0 / jnp.maximum(jnp.abs(den), jnp.exp(-(b_col + mt_u))))

        m_new = b_all + jnp.maximum(m_old, jnp.max(u_row, axis=1, keepdims=True))
        decay = jnp.exp(b_all + m_old - m_new)
        kw_t = kt * jnp.exp(b_all + u_row - m_new).astype(BF16)
        c_scr[d] = decay * c_old + jnp.dot(kw_t, v_aug, preferred_element_type=F32)
        m_scr[d] = m_new
        return h

    cf = step
    cb = nc - 1 - step
    rf = pl.multiple_of(cf * chunk, chunk)
    rb = pl.multiple_of(cb * chunk, chunk)
    hf_scr[pl.ds(rf, chunk), :] = scan_chunk(0, qf_ref, kf_ref, vf_ref, gcf_ref, grf_ref)
    hb_scr[pl.ds(rb, chunk), :] = scan_chunk(1, qb_ref, kb_ref, vb_ref, gcb_ref, grb_ref)

    def finalize(r0):
        hm = hf_scr[pl.ds(r0, chunk), :] + hb_scr[pl.ds(r0, chunk), :]
        hm = hm * lax.rsqrt(jnp.mean(hm * hm, axis=-1, keepdims=True) + EPS)
        gate = _sigmoid(o_ref[pl.ds(r0, chunk), :])
        zg = _silu(z_ref[pl.ds(r0, chunk), :])
        out_ref[pl.ds(r0, chunk), :] = hm.astype(BF16) * hw_ref[...].astype(BF16) * gate * zg

    if nc == 1:
        finalize(0)
    else:
        @pl.when(step >= nc // 2)
        def _():
            finalize(rf)
            finalize(rb)

    if has_state_out:
        @pl.when(step == nc - 1)
        def _():
            for d in range(2):
                cn_ref[d] = c_scr[d, :, :dv]
                nn_ref[d] = c_scr[d, :, dv:dv + 1]
                mn_ref[d] = m_scr[d]


def _mlstm(proj, k_t, gc, gr, hnorm_w, init, heads, dqk, dv, want_state):
    bsz, t, n_proj = proj.shape
    chunk = min(MLSTM_CHUNK, t)
    nc = t // chunk
    assert nc == 1 or nc % 2 == 0
    q_blk = (n_proj - heads * dqk) // dqk
    tri = np.tril(np.ones((chunk, chunk), bool))
    low = jnp.asarray(np.where(tri, 0.0, -np.inf).astype(np.float32))
    upp = jnp.asarray(np.where(tri.T, 0.0, -np.inf).astype(np.float32))

    def scan_specs(ci):
        return [
            pl.BlockSpec((None, chunk, dqk), lambda b, h, s: (b, ci(s), q_blk + h)),
            pl.BlockSpec((dqk, chunk), lambda b, h, s: (h, b * nc + ci(s))),
            pl.BlockSpec((None, chunk, dv), lambda b, h, s: (b, ci(s), h)),
            pl.BlockSpec((None, chunk, GATE_SLOTS), lambda b, h, s: (h, b * nc + ci(s), 0)),
            pl.BlockSpec((GATE_SLOTS, chunk), lambda b, h, s: (h, b * nc + ci(s))),
        ]

    in_specs = scan_specs(lambda s: s) + scan_specs(lambda s: nc - 1 - s) + [
        pl.BlockSpec((None, t, dv), lambda b, h, s: (b, 0, heads + h)),
        pl.BlockSpec((None, t, dv), lambda b, h, s: (b, 0, 2 * heads + h)),
        pl.BlockSpec((1, dv), lambda b, h, s: (0, h)),
        pl.BlockSpec((chunk, chunk), lambda b, h, s: (0, 0)),
        pl.BlockSpec((chunk, chunk), lambda b, h, s: (0, 0)),
    ]
    scan_args = [proj, k_t, proj, gc, gr]
    args = scan_args + scan_args + [proj, proj, hnorm_w, low, upp]
    state_specs = [
        pl.BlockSpec((None, 2, None, dqk, dv), lambda b, h, s: (b, 0, h, 0, 0)),
        pl.BlockSpec((None, 2, None, dqk, 1), lambda b, h, s: (b, 0, h, 0, 0)),
        pl.BlockSpec((None, 2, None, 1, 1), lambda b, h, s: (b, 0, h, 0, 0)),
    ]
    if init is not None:
        in_specs += state_specs
        args += list(init)
    out_specs = [pl.BlockSpec((None, t, dv), lambda b, h, s: (b, 0, h))]
    out_shape = [jax.ShapeDtypeStruct((bsz, t, heads * dv), BF16)]
    if want_state:
        out_specs += state_specs
        out_shape += [
            jax.ShapeDtypeStruct((bsz, 2, heads, dqk, dv), F32),
            jax.ShapeDtypeStruct((bsz, 2, heads, dqk, 1), F32),
            jax.ShapeDtypeStruct((bsz, 2, heads, 1, 1), F32),
        ]
    body = functools.partial(_mlstm_body, nc=nc, has_init=init is not None,
                             has_state_out=want_state, k_scale=dqk ** -0.5)
    return pl.pallas_call(
        body,
        grid=(bsz, heads, nc),
        in_specs=in_specs,
        out_specs=out_specs,
        out_shape=out_shape,
        scratch_shapes=[pltpu.VMEM((2, dqk, dv + LANES), F32), pltpu.VMEM((2, 1, 1), F32),
                        pltpu.VMEM((t, dv), F32), pltpu.VMEM((t, dv), F32)],
        compiler_params=_params("arbitrary", "arbitrary", "arbitrary"),
        name="mlstm",
    )(*args)


def _dft_cos_sin(n):
    ang = 2.0 * np.pi * np.outer(np.arange(n), np.arange(n)) / n
    return np.cos(ang), np.sin(ang)


def _four_ctx_body(u_ref, z_ref, cc_ref, sc_ref, ct_ref, st_ref, w4_ref, o_ref, *, scale):
    cg = cc_ref.shape[0]
    for g in range(w4_ref.shape[0]):
        cols = slice(g * cg, (g + 1) * cg)
        x = u_ref[:, cols]
        a = jnp.dot(x, cc_ref[...], preferred_element_type=F32).astype(BF16)
        b = jnp.dot(x, sc_ref[...], preferred_element_type=F32).astype(BF16)
        mixed = (jnp.dot(ct_ref[...], a, preferred_element_type=F32)
                 - jnp.dot(st_ref[...], b, preferred_element_type=F32)) * scale
        f = jnp.dot(mixed.astype(BF16), w4_ref[g], preferred_element_type=F32)
        o_ref[:, cols] = (f * _silu(z_ref[:, cols].astype(F32))).astype(o_ref.dtype)


def _fourier_ctx(proj, w4, groups, cg, u_blk, z_blk):
    bsz, t, _ = proj.shape
    fw = groups * cg
    assert (u_blk * cg) % fw == 0 and (z_blk * cg) % fw == 0
    cc, sc = _dft_cos_sin(cg)
    ct, st = _dft_cos_sin(t)
    consts = [jnp.asarray(m, F32).astype(BF16) for m in (cc, sc, ct, st)]
    body = functools.partial(_four_ctx_body, scale=float(1.0 / np.sqrt(t * cg)))
    sq = lambda n: pl.BlockSpec((n, n), lambda b: (0, 0))
    return pl.pallas_call(
        body,
        grid=(bsz,),
        in_specs=[pl.BlockSpec((None, t, fw), lambda b: (b, 0, u_blk * cg // fw)),
                  pl.BlockSpec((None, t, fw), lambda b: (b, 0, z_blk * cg // fw)),
                  sq(cg), sq(cg), sq(t), sq(t),
                  pl.BlockSpec((groups, cg, cg), lambda b: (0, 0, 0))],
        out_specs=pl.BlockSpec((None, t, fw), lambda b: (b, 0, 0)),
        out_shape=jax.ShapeDtypeStruct((bsz, t, fw), BF16),
        compiler_params=_params("arbitrary"),
        name="fourier_ctx",
    )(proj, proj, *consts, w4)


def _four_lat_body(u_ref, z_ref, cc_ref, sc_ref, wpq_ref, kr_ref, w4_ref, o_ref,
                   p_scr, q_scr, f_scr, *, scale, tile, oct_w):
    rows, width, cg = p_scr.shape
    t = rows * width
    rows_per_tile = tile // width

    def width_stage(i, carry):
        r0 = pl.multiple_of(i * tile, tile)
        x = u_ref[pl.ds(r0, tile), :]
        a = jnp.dot(x, cc_ref[...], preferred_element_type=F32).astype(BF16)
        b = jnp.dot(x, sc_ref[...], preferred_element_type=F32).astype(BF16)
        ab = jnp.concatenate([a, b], axis=0)
        pq = jnp.dot(wpq_ref[...], ab, preferred_element_type=F32)
        g0 = pl.multiple_of(i * rows_per_tile, rows_per_tile)
        p_scr[pl.ds(g0, rows_per_tile), :, :] = pq[:tile].reshape(rows_per_tile, width, cg)
        q_scr[pl.ds(g0, rows_per_tile), :, :] = pq[tile:].reshape(rows_per_tile, width, cg)
        return carry

    lax.fori_loop(0, t // tile, width_stage, 0, unroll=4)

    def row_stage(j, carry):
        w0 = pl.multiple_of(j * oct_w, oct_w)
        pj = p_scr[:, pl.ds(w0, oct_w), :].reshape(rows * oct_w, cg).astype(BF16)
        qj = q_scr[:, pl.ds(w0, oct_w), :].reshape(rows * oct_w, cg).astype(BF16)
        pq = jnp.concatenate([pj, qj], axis=0)
        mixed = jnp.dot(kr_ref[...], pq, preferred_element_type=F32) * scale
        f = jnp.dot(mixed.astype(BF16), w4_ref[...], preferred_element_type=F32)
        f_scr[:, pl.ds(w0, oct_w), :] = f.reshape(rows, oct_w, cg)
        return carry

    lax.fori_loop(0, width // oct_w, row_stage, 0, unroll=4)

    def gate_stage(i, carry):
        r0 = pl.multiple_of(i * tile, tile)
        g0 = pl.multiple_of(i * rows_per_tile, rows_per_tile)
        f = f_scr[pl.ds(g0, rows_per_tile), :, :].reshape(tile, cg)
        z = z_ref[pl.ds(r0, tile), :].astype(F32)
        o_ref[pl.ds(r0, tile), :] = (f * _silu(z)).astype(o_ref.dtype)
        return carry

    lax.fori_loop(0, t // tile, gate_stage, 0)


def _fourier_lat(proj, w4, groups, cg, u_blk, z_blk):
    bsz, t, _ = proj.shape
    width = GRID_W
    rows = t // width
    tile = 256
    oct_w = 8
    cc, sc = _dft_cos_sin(cg)
    cw, sw = _dft_cos_sin(width)
    cr, sr = _dft_cos_sin(rows)
    eye_t = np.eye(tile // width)
    bwc, bws = np.kron(eye_t, cw), np.kron(eye_t, sw)
    wpq = np.block([[bwc, -bws], [bws, bwc]])
    eye_o = np.eye(oct_w)
    kr = np.concatenate([np.kron(cr, eye_o), -np.kron(sr, eye_o)], axis=1)
    consts = [jnp.asarray(m, F32).astype(BF16) for m in (cc, sc, wpq, kr)]
    body = functools.partial(_four_lat_body, scale=float(1.0 / np.sqrt(t * cg)), tile=tile, oct_w=oct_w)
    full = lambda a: pl.BlockSpec(a.shape, lambda b, g: (0, 0))
    return pl.pallas_call(
        body,
        grid=(bsz, groups),
        in_specs=[pl.BlockSpec((None, t, cg), lambda b, g: (b, 0, u_blk + g)),
                  pl.BlockSpec((None, t, cg), lambda b, g: (b, 0, z_blk + g))]
                 + [full(a) for a in consts]
                 + [pl.BlockSpec((None, cg, cg), lambda b, g: (g, 0, 0))],
        out_specs=pl.BlockSpec((None, t, cg), lambda b, g: (b, 0, g)),
        out_shape=jax.ShapeDtypeStruct((bsz, t, groups * cg), BF16),
        scratch_shapes=[pltpu.VMEM((rows, width, cg), F32)] * 3,
        compiler_params=_params("arbitrary", "arbitrary"),
        name="fourier_lat",
    )(proj, proj, *consts, w4)


def _out_body(m_ref, f_ref, w_ref, x_ref, gate_ref, fw_ref, o_ref, *, tn):
    mw = m_ref.shape[1]
    d = o_ref.shape[1]
    m = m_ref[...]
    f = f_ref[...]
    ssq = jnp.zeros((o_ref.shape[0], 1), F32)
    for n in range(d // tn):
        cols = slice(n * tn, (n + 1) * tn)
        acc = (jnp.dot(m, w_ref[:mw, cols], preferred_element_type=F32)
               + jnp.dot(f, w_ref[mw:, cols], preferred_element_type=F32))
        y = x_ref[:, cols] + gate_ref[:, cols] * acc
        ssq = ssq + jnp.sum(y * y, axis=-1, keepdims=True)
        o_ref[:, cols] = y
    inv = lax.rsqrt(ssq * (1.0 / d) + EPS)
    for n in range(d // tn):
        cols = slice(n * tn, (n + 1) * tn)
        o_ref[:, cols] = o_ref[:, cols] * inv * fw_ref[:, cols]


def _out_proj(m_out, f_out, w_out, x, mod, mod_row, final_w):
    bsz, t, d = x.shape
    mw = m_out.shape[-1]
    tm = min(256, t)
    tiles = t // tm
    m2 = m_out.reshape(bsz * t, mw)
    f2 = f_out.reshape(bsz * t, f_out.shape[-1])
    x2 = x.reshape(bsz * t, d)
    y = pl.pallas_call(
        functools.partial(_out_body, tn=min(512, d)),
        grid=(bsz * tiles,),
        in_specs=[pl.BlockSpec((tm, mw), lambda i: (i, 0)),
                  pl.BlockSpec((tm, f2.shape[1]), lambda i: (i, 0)),
                  pl.BlockSpec(w_out.shape, lambda i: (0, 0), pipeline_mode=pl.Buffered(1)),
                  pl.BlockSpec((tm, d), lambda i: (i, 0)),
                  pl.BlockSpec((None, 1, d), lambda i: (mod_row(i // tiles) * 3 + 2, 0, 0)),
                  pl.BlockSpec((1, d), lambda i: (0, 0))],
        out_specs=pl.BlockSpec((tm, d), lambda i: (i, 0)),
        out_shape=jax.ShapeDtypeStruct((bsz * t, d), F32),
        compiler_params=_params("arbitrary"),
        name="out_proj",
    )(m2, f2, w_out, x2, mod, final_w)
    return y.reshape(bsz, t, d)


def _layer(x, mod, mod_row, norm_w, w_main, wg, bg, hnorm_w, w4, w_out, final_w,
           init, want_state, latent, heads, groups):
    bsz, t, d = x.shape
    mw = d // 2
    dv = mw // heads
    dqk = dv // 2
    cg = (d - mw) // groups
    tm = min(1024, bsz * t)
    assert t % tm == 0 or tm % t == 0
    tile_mod_row = lambda p: mod_row((p * tm) // t)
    k_t, proj, gc, gr = _norm_proj(x.reshape(bsz * t, d), mod, tile_mod_row, norm_w, w_main, heads * dqk,
                                   wg, bg, heads, min(MLSTM_CHUNK, t))
    proj = proj.reshape(bsz, t, -1)
    res = _mlstm(proj, k_t, gc, gr, hnorm_w, init, heads, dqk, dv, want_state)
    u_blk = 3 * mw // cg
    z_blk = (3 * mw + (d - mw)) // cg
    four = _fourier_lat if latent else _fourier_ctx
    f_out = four(proj, w4, groups, cg, u_blk, z_blk)
    y = _out_proj(res[0], f_out, w_out, x, mod, mod_row, final_w)
    return y, res[1:]


def kernel(x_prompt, x_sample, c, state_C, state_n, state_m, c_ctx, w_ada, b_ada, norm_w, w_in, b_gates, hnorm_w, w_four, w_out, final_norm_w):
    depth = w_ada.shape[0]
    assert depth == 1, "single-layer step only"
    bp, _, d = x_prompt.shape
    bs = x_sample.shape[0]
    heads = state_C.shape[3]
    groups = w_four.shape[1]
    mw = d // 2
    n_gates = 4 * heads
    gate0 = 4 * mw
    gw = heads * GATE_SLOTS
    assert 1 + bs <= MOD_ROWS and gw <= LANES

    cvec = jnp.zeros((MOD_ROWS, d), F32).at[0].set(c_ctx).at[1:1 + bs].set(c)
    mod = _modulation(cvec, w_ada[0], b_ada).reshape(MOD_ROWS * 3, 1, d)

    w_t = jnp.swapaxes(w_in, 1, 2)[0]
    w_main = _prep_w_in(w_t, gate0, n_gates, mw // 2)
    w_g = w_t[gate0:gate0 + n_gates].T.reshape(d, 4, heads).transpose(0, 2, 1)
    w_g = jnp.pad(w_g, ((0, 0), (0, 0), (0, GATE_SLOTS - 4))).reshape(d, gw)
    wg = jnp.pad(w_g, ((0, 0), (0, LANES - gw))).astype(BF16)
    b_g = jnp.pad(b_gates[0].reshape(4, heads).T, ((0, 0), (0, GATE_SLOTS - 4))).reshape(1, gw)
    bg = jnp.pad(b_g, ((0, 0), (0, LANES - gw))).astype(F32)
    w4 = w_four[0].astype(BF16)
    wo = w_out[0].astype(BF16)
    shared = (norm_w, w_main, wg, bg, hnorm_w, w4, wo, final_norm_w.reshape(1, d))

    y_prompt, (c_new, n_new, m_new) = _layer(
        x_prompt, mod, lambda b: b * 0, *shared, None, True, False, heads, groups)
    init = (state_C[:, 0], state_n[:, 0][..., None], state_m[:, 0][..., None, None])
    y_sample, _ = _layer(
        x_sample, mod, lambda b: b + 1, *shared, init, False, True, heads, groups)

    new_c = c_new[:, None]
    new_n = n_new.reshape(bp, 1, 2, heads, -1)
    new_m = m_new.reshape(bp, 1, 2, heads)
    return (y_prompt, y_sample, new_c, new_n, new_m)
```

```python
import functools

import numpy as np
import jax
import jax.numpy as jnp
from jax import lax
from jax.experimental import pallas as pl
from jax.experimental.pallas import tpu as pltpu

F32 = jnp.float32
BF16 = jnp.bfloat16

EPS = 1e-6
GRID_W = 64
MLSTM_CHUNK = 256
GATE_SLOTS = 8
LANES = 128
MOD_ROWS = 16
V7X_VMEM_BYTES = 64 * 1024 * 1024
VMEM_LIMIT = V7X_VMEM_BYTES - 4 * 1024 * 1024


def _params(*sem):
    return pltpu.CompilerParams(dimension_semantics=sem, vmem_limit_bytes=VMEM_LIMIT)


def _log_sigmoid(x):
    return jnp.minimum(x, 0.0) - jnp.log1p(jnp.exp(-jnp.abs(x)))


def _sigmoid(x):
    return 0.5 * jnp.tanh(0.5 * x) + 0.5


def _silu(x):
    return x * _sigmoid(x)


def _mod_body(c_ref, w_ref, b_ref, o_ref):
    s = _silu(c_ref[...]).astype(BF16)
    o_ref[...] = jnp.dot(s, w_ref[...].astype(BF16), preferred_element_type=F32) + b_ref[...]


def _modulation(cvec, w_ada, b_ada):
    rows, d = cvec.shape
    n = w_ada.shape[1]
    tn = min(512, n)
    return pl.pallas_call(
        _mod_body,
        grid=(n // tn,),
        in_specs=[pl.BlockSpec((rows, d), lambda j: (0, 0)),
                  pl.BlockSpec((d, tn), lambda j: (0, j)),
                  pl.BlockSpec((1, tn), lambda j: (0, j))],
        out_specs=pl.BlockSpec((rows, tn), lambda j: (0, j)),
        out_shape=jax.ShapeDtypeStruct((rows, n), F32),
        compiler_params=_params("arbitrary"),
        name="modulation",
    )(cvec, w_ada, b_ada)


def _wprep_body(w_ref, o_ref):
    o_ref[...] = w_ref[...].astype(BF16)


def _prep_w_in(w_t, gate0, n_gates, qk_rows):
    n, d = w_t.shape
    tr = 256
    assert qk_rows % tr == 0 and gate0 % tr == 0 and (n - gate0 - n_gates) % tr == 0 and n_gates % 8 == 0
    q_dst = n - n_gates - qk_rows

    def src_row(j):
        r = j * tr
        src = jnp.where(r < gate0 - qk_rows, r + qk_rows,
                        jnp.where(r < q_dst, r + qk_rows + n_gates, r - q_dst))
        return pl.multiple_of(src, 8)

    return pl.pallas_call(
        _wprep_body,
        grid=((n - n_gates) // tr,),
        in_specs=[pl.BlockSpec((pl.Element(tr), pl.Element(d)), lambda j: (src_row(j), 0))],
        out_specs=pl.BlockSpec((tr, d), lambda j: (j, 0)),
        out_shape=jax.ShapeDtypeStruct((n - n_gates, d), BF16),
        compiler_params=_params("arbitrary"),
        name="w_in_cast",
    )(w_t)


def _split3(x):
    hi = x.astype(BF16)
    r1 = x - hi.astype(F32)
    mid = r1.astype(BF16)
    lo = (r1 - mid.astype(F32)).astype(BF16)
    return hi, mid, lo


def _prenorm_body(x_ref, sh_ref, sc_ref, nw_ref, wg_ref, bg_ref, low_ref, upp_ref,
                  h_ref, gc_ref, gr_ref, inv_scr, mul_scr, add_scr, *, heads, chunk, sub_ssq, sub_norm):
    tm = x_ref.shape[0]

    def ssq_step(i, carry):
        r0 = pl.multiple_of(i * sub_ssq, sub_ssq)
        ms = jnp.mean(jnp.square(x_ref[pl.ds(r0, sub_ssq), :]), axis=-1, keepdims=True)
        inv_scr[pl.ds(r0, sub_ssq), :] = jnp.broadcast_to(lax.rsqrt(ms + EPS), (sub_ssq, LANES))
        return carry

    lax.fori_loop(0, tm // sub_ssq, ssq_step, 0)
    mul_scr[...] = jnp.broadcast_to(nw_ref[...] * (1.0 + sc_ref[...]), mul_scr.shape)
    add_scr[...] = jnp.broadcast_to(sh_ref[...], add_scr.shape)

    def norm_step(i, carry):
        r0 = pl.multiple_of(i * sub_norm, sub_norm)
        inv = inv_scr[pl.ds(r0, sub_norm), :]
        for lt in range(x_ref.shape[1] // LANES):
            cols = slice(lt * LANES, (lt + 1) * LANES)
            y = x_ref[pl.ds(r0, sub_norm), cols] * inv
            h_ref[pl.ds(r0, sub_norm), cols] = (y * mul_scr[:, cols] + add_scr[:, cols]).astype(BF16)
        return carry

    lax.fori_loop(0, tm // sub_norm, norm_step, 0, unroll=2)

    kh = x_ref.shape[1] // 2
    g = (jnp.dot(h_ref[:, :kh], wg_ref[:kh, :], preferred_element_type=F32)
         + jnp.dot(h_ref[:, kh:], wg_ref[kh:, :], preferred_element_type=F32)) + bg_ref[...]
    slot = lax.broadcasted_iota(jnp.int32, (chunk, LANES), 1) % GATE_SLOTS
    for c in range(tm // chunk):
        gch = g[c * chunk:(c + 1) * chunk]
        lf = jnp.where((slot == 1) | (slot == 3), _log_sigmoid(gch), 0.0)
        hi, mid, lo = _split3(lf)
        lf3 = jnp.concatenate([hi, mid, lo], axis=1)
        pre = jnp.dot(low_ref[...], lf3, preferred_element_type=F32)
        suf = jnp.dot(upp_ref[...], lf3, preferred_element_type=F32)
        pre = pre[:, :LANES] + pre[:, LANES:2 * LANES] + pre[:, 2 * LANES:]
        suf = suf[:, :LANES] + suf[:, LANES:2 * LANES] + suf[:, 2 * LANES:]
        b = jnp.where(slot == 1, pre, jnp.where(slot == 3, suf, 0.0))
        u = gch - pltpu.roll(b, LANES - 1, 1)
        res = jnp.where((slot == 0) | (slot == 2), u, b)
        for hd in range(heads):
            gc_ref[hd, c * chunk:(c + 1) * chunk, :] = res[:, hd * GATE_SLOTS:(hd + 1) * GATE_SLOTS]
        gr_ref[:, c * chunk:(c + 1) * chunk] = res.T[:heads * GATE_SLOTS, :]


def _prenorm(x, mod, mod_row, norm_w, wg, bg, heads, chunk):
    bsz, t, d = x.shape
    tm = min(512, t)
    assert tm % chunk == 0
    sub_norm = min(16, tm)
    gw = heads * GATE_SLOTS
    tri = np.tril(np.ones((chunk, chunk), np.float32))
    low, upp = jnp.asarray(tri).astype(BF16), jnp.asarray(tri.T).astype(BF16)
    body = functools.partial(_prenorm_body, heads=heads, chunk=chunk,
                             sub_ssq=min(64, tm), sub_norm=sub_norm)
    return pl.pallas_call(
        body,
        grid=(bsz, t // tm),
        in_specs=[pl.BlockSpec((None, tm, d), lambda b, i: (b, i, 0)),
                  pl.BlockSpec((None, 1, d), lambda b, i: (mod_row(b) * 3, 0, 0)),
                  pl.BlockSpec((None, 1, d), lambda b, i: (mod_row(b) * 3 + 1, 0, 0)),
                  pl.BlockSpec((1, d), lambda b, i: (0, 0)),
                  pl.BlockSpec((d, LANES), lambda b, i: (0, 0)),
                  pl.BlockSpec((1, LANES), lambda b, i: (0, 0)),
                  pl.BlockSpec((chunk, chunk), lambda b, i: (0, 0)),
                  pl.BlockSpec((chunk, chunk), lambda b, i: (0, 0))],
        out_specs=[pl.BlockSpec((None, tm, d), lambda b, i: (b, i, 0)),
                   pl.BlockSpec((None, heads, tm, GATE_SLOTS), lambda b, i: (b, 0, i, 0)),
                   pl.BlockSpec((None, gw, tm), lambda b, i: (b, 0, i))],
        out_shape=[jax.ShapeDtypeStruct((bsz, t, d), BF16),
                   jax.ShapeDtypeStruct((bsz, heads, t, GATE_SLOTS), F32),
                   jax.ShapeDtypeStruct((bsz, gw, t), F32)],
        scratch_shapes=[pltpu.VMEM((tm, LANES), F32), pltpu.VMEM((sub_norm, d), F32),
                        pltpu.VMEM((sub_norm, d), F32)],
        compiler_params=_params("arbitrary", "arbitrary"),
        name="prenorm",
    )(x, mod, mod, norm_w, wg, bg, low, upp)


def _in_proj_body(a_ref, w_ref, kt_ref, o_ref):
    j = pl.program_id(1)
    nt = (((1,), (1,)), ((), ()))

    @pl.when(j == 0)
    def _():
        kt_ref[...] = lax.dot_general(w_ref[...], a_ref[...], nt,
                                      preferred_element_type=F32).astype(kt_ref.dtype)

    @pl.when(j > 0)
    def _():
        o_ref[...] = lax.dot_general(a_ref[...], w_ref[...], nt,
                                     preferred_element_type=F32).astype(o_ref.dtype)


def _in_proj(h, w_t, k_rows):
    m, k = h.shape
    n = w_t.shape[0]
    tm, tn = min(1024, m), k_rows
    assert n % tn == 0 and tn % 256 == 0
    return pl.pallas_call(
        _in_proj_body,
        grid=(m // tm, n // tn),
        in_specs=[pl.BlockSpec((tm, k), lambda i, j: (i, 0)),
                  pl.BlockSpec((tn, k), lambda i, j: (j, 0))],
        out_specs=[pl.BlockSpec((tn, tm), lambda i, j: (0, i)),
                   pl.BlockSpec((tm, tn), lambda i, j: (i, jnp.maximum(j - 1, 0)))],
        out_shape=[jax.ShapeDtypeStruct((k_rows, m), BF16),
                   jax.ShapeDtypeStruct((m, n - k_rows), BF16)],
        compiler_params=_params("arbitrary", "arbitrary"),
        name="in_proj",
    )(h, w_t)


def _mlstm_body(*refs, nc, has_init, has_state_out, k_scale):
    (qf_ref, kf_ref, vf_ref, gcf_ref, grf_ref, qb_ref, kb_ref, vb_ref, gcb_ref, grb_ref,
     o_ref, z_ref, hw_ref, lown_ref, uppn_ref) = refs[:15]
    pos = 15
    if has_init:
        c0_ref, n0_ref, m0_ref = refs[pos:pos + 3]
        pos += 3
    out_ref = refs[pos]
    pos += 1
    if has_state_out:
        cn_ref, nn_ref, mn_ref = refs[pos:pos + 3]
        pos += 3
    c_scr, m_scr, hf_scr, hb_scr = refs[pos:]
    step = pl.program_id(2)
    chunk, dqk = qf_ref.shape
    dv = vf_ref.shape[1]

    @pl.when(step == 0)
    def _():
        for d in range(2):
            if has_init:
                c_scr[d, :, :dv] = c0_ref[d]
                c_scr[d, :, dv:] = jnp.broadcast_to(n0_ref[d], (dqk, LANES))
                m_scr[d] = m0_ref[d]
            else:
                c_scr[d] = jnp.zeros(c_scr.shape[1:], F32)
                m_scr[d] = jnp.zeros(m_scr.shape[1:], F32)

    ones_tile = jnp.ones((chunk, LANES), BF16)

    def scan_chunk(d, q_ref, k_ref, v_ref, gc_ref, gr_ref):
        q = q_ref[...]
        kt = k_ref[...] * jnp.asarray(k_scale, BF16)
        qk = jnp.dot(q, kt, preferred_element_type=F32)
        v_aug = jnp.concatenate([v_ref[...], ones_tile], axis=1)
        b_col = gc_ref[:, 2 * d + 1:2 * d + 2]
        u_row = gr_ref[2 * d:2 * d + 1, :]
        b_all = b_col[chunk - 1:chunk] if d == 0 else b_col[0:1]
        m_old = m_scr[d]
        c_old = c_scr[d]

        um = u_row + (lown_ref if d == 0 else uppn_ref)[...]
        mt_u = jnp.maximum(m_old, jnp.max(um, axis=1, keepdims=True))
        sm = (qk * jnp.exp(um - mt_u)).astype(BF16)
        q_in = q * jnp.exp(m_old - mt_u).astype(BF16)
        nd = (jnp.dot(sm, v_aug, preferred_element_type=F32)
              + jnp.dot(q_in, c_old.astype(BF16), preferred_element_type=F32))
        den = nd[:, dv:dv + 1]
        h = nd[:, :dv] * (1.0 / jnp.maximum(jnp.abs(den), jnp.exp(-(b_col + mt_u))))

        m_new = b_all + jnp.maximum(m_old, jnp.max(u_row, axis=1, keepdims=True))
        decay = jnp.exp(b_all + m_old - m_new)
        kw_t = kt * jnp.exp(b_all + u_row - m_new).astype(BF16)
        c_scr[d] = decay * c_old + jnp.dot(kw_t, v_aug, preferred_element_type=F32)
        m_scr[d] = m_new
        return h

    cf = step
    cb = nc - 1 - step
    rf = pl.multiple_of(cf * chunk, chunk)
    rb = pl.multiple_of(cb * chunk, chunk)
    hf_scr[pl.ds(rf, chunk), :] = scan_chunk(0, qf_ref, kf_ref, vf_ref, gcf_ref, grf_ref)
    hb_scr[pl.ds(rb, chunk), :] = scan_chunk(1, qb_ref, kb_ref, vb_ref, gcb_ref, grb_ref)

    def finalize(r0):
        hm = hf_scr[pl.ds(r0, chunk), :] + hb_scr[pl.ds(r0, chunk), :]
        hm = hm * lax.rsqrt(jnp.mean(hm * hm, axis=-1, keepdims=True) + EPS)
        gate = _sigmoid(o_ref[pl.ds(r0, chunk), :])
        zg = _silu(z_ref[pl.ds(r0, chunk), :])
        out_ref[pl.ds(r0, chunk), :] = hm.astype(BF16) * hw_ref[...].astype(BF16) * gate * zg

    if nc == 1:
        finalize(0)
    else:
        @pl.when(step >= nc // 2)
        def _():
            finalize(rf)
            finalize(rb)

    if has_state_out:
        @pl.when(step == nc - 1)
        def _():
            for d in range(2):
                cn_ref[d] = c_scr[d, :, :dv]
                nn_ref[d] = c_scr[d, :, dv:dv + 1]
                mn_ref[d] = m_scr[d]


def _mlstm(proj, k_t, gc, gr, hnorm_w, init, heads, dqk, dv, want_state):
    bsz, t, n_proj = proj.shape
    chunk = min(MLSTM_CHUNK, t)
    nc = t // chunk
    assert nc == 1 or nc % 2 == 0
    q_blk = (n_proj - heads * dqk) // dqk
    tri = np.tril(np.ones((chunk, chunk), bool))
    low = jnp.asarray(np.where(tri, 0.0, -np.inf).astype(np.float32))
    upp = jnp.asarray(np.where(tri.T, 0.0, -np.inf).astype(np.float32))

    def scan_specs(ci):
        return [
            pl.BlockSpec((None, chunk, dqk), lambda b, h, s: (b, ci(s), q_blk + h)),
            pl.BlockSpec((dqk, chunk), lambda b, h, s: (h, b * nc + ci(s))),
            pl.BlockSpec((None, chunk, dv), lambda b, h, s: (b, ci(s), h)),
            pl.BlockSpec((None, None, chunk, GATE_SLOTS), lambda b, h, s: (b, h, ci(s), 0)),
            pl.BlockSpec((None, GATE_SLOTS, chunk), lambda b, h, s: (b, h, ci(s))),
        ]

    in_specs = scan_specs(lambda s: s) + scan_specs(lambda s: nc - 1 - s) + [
        pl.BlockSpec((None, t, dv), lambda b, h, s: (b, 0, heads + h)),
        pl.BlockSpec((None, t, dv), lambda b, h, s: (b, 0, 2 * heads + h)),
        pl.BlockSpec((1, dv), lambda b, h, s: (0, h)),
        pl.BlockSpec((chunk, chunk), lambda b, h, s: (0, 0)),
        pl.BlockSpec((chunk, chunk), lambda b, h, s: (0, 0)),
    ]
    scan_args = [proj, k_t, proj, gc, gr]
    args = scan_args + scan_args + [proj, proj, hnorm_w, low, upp]
    state_specs = [
        pl.BlockSpec((None, 2, None, dqk, dv), lambda b, h, s: (b, 0, h, 0, 0)),
        pl.BlockSpec((None, 2, None, dqk, 1), lambda b, h, s: (b, 0, h, 0, 0)),
        pl.BlockSpec((None, 2, None, 1, 1), lambda b, h, s: (b, 0, h, 0, 0)),
    ]
    if init is not None:
        in_specs += state_specs
        args += list(init)
    out_specs = [pl.BlockSpec((None, t, dv), lambda b, h, s: (b, 0, h))]
    out_shape = [jax.ShapeDtypeStruct((bsz, t, heads * dv), BF16)]
    if want_state:
        out_specs += state_specs
        out_shape += [
            jax.ShapeDtypeStruct((bsz, 2, heads, dqk, dv), F32),
            jax.ShapeDtypeStruct((bsz, 2, heads, dqk, 1), F32),
            jax.ShapeDtypeStruct((bsz, 2, heads, 1, 1), F32),
        ]
    body = functools.partial(_mlstm_body, nc=nc, has_init=init is not None,
                             has_state_out=want_state, k_scale=dqk ** -0.5)
    return pl.pallas_call(
        body,
        grid=(bsz, heads, nc),
        in_specs=in_specs,
        out_specs=out_specs,
        out_shape=out_shape,
        scratch_shapes=[pltpu.VMEM((2, dqk, dv + LANES), F32), pltpu.VMEM((2, 1, 1), F32),
                        pltpu.VMEM((t, dv), F32), pltpu.VMEM((t, dv), F32)],
        compiler_params=_params("arbitrary", "arbitrary", "arbitrary"),
        name="mlstm",
    )(*args)


def _dft_cos_sin(n):
    ang = 2.0 * np.pi * np.outer(np.arange(n), np.arange(n)) / n
    return np.cos(ang), np.sin(ang)


def _four_ctx_body(u_ref, z_ref, cc_ref, sc_ref, ct_ref, st_ref, w4_ref, o_ref, *, scale):
    cg = cc_ref.shape[0]
    for g in range(w4_ref.shape[0]):
        cols = slice(g * cg, (g + 1) * cg)
        x = u_ref[:, cols]
        a = jnp.dot(x, cc_ref[...], preferred_element_type=F32).astype(BF16)
        b = jnp.dot(x, sc_ref[...], preferred_element_type=F32).astype(BF16)
        mixed = (jnp.dot(ct_ref[...], a, preferred_element_type=F32)
                 - jnp.dot(st_ref[...], b, preferred_element_type=F32)) * scale
        f = jnp.dot(mixed.astype(BF16), w4_ref[g], preferred_element_type=F32)
        o_ref[:, cols] = (f * _silu(z_ref[:, cols].astype(F32))).astype(o_ref.dtype)


def _fourier_ctx(proj, w4, groups, cg, u_blk, z_blk):
    bsz, t, _ = proj.shape
    fw = groups * cg
    assert (u_blk * cg) % fw == 0 and (z_blk * cg) % fw == 0
    cc, sc = _dft_cos_sin(cg)
    ct, st = _dft_cos_sin(t)
    consts = [jnp.asarray(m, F32).astype(BF16) for m in (cc, sc, ct, st)]
    body = functools.partial(_four_ctx_body, scale=float(1.0 / np.sqrt(t * cg)))
    sq = lambda n: pl.BlockSpec((n, n), lambda b: (0, 0))
    return pl.pallas_call(
        body,
        grid=(bsz,),
        in_specs=[pl.BlockSpec((None, t, fw), lambda b: (b, 0, u_blk * cg // fw)),
                  pl.BlockSpec((None, t, fw), lambda b: (b, 0, z_blk * cg // fw)),
                  sq(cg), sq(cg), sq(t), sq(t),
                  pl.BlockSpec((groups, cg, cg), lambda b: (0, 0, 0))],
        out_specs=pl.BlockSpec((None, t, fw), lambda b: (b, 0, 0)),
        out_shape=jax.ShapeDtypeStruct((bsz, t, fw), BF16),
        compiler_params=_params("arbitrary"),
        name="fourier_ctx",
    )(proj, proj, *consts, w4)


def _four_lat_body(u_ref, z_ref, cc_ref, sc_ref, wpq_ref, krc_ref, krs_ref, w4_ref, o_ref,
                   p_scr, q_scr, f_scr, *, scale, tile, oct_w):
    rows, width, cg = p_scr.shape
    t = rows * width
    rows_per_tile = tile // width

    def width_stage(i, carry):
        r0 = pl.multiple_of(i * tile, tile)
        x = u_ref[pl.ds(r0, tile), :]
        a = jnp.dot(x, cc_ref[...], preferred_element_type=F32).astype(BF16)
        b = jnp.dot(x, sc_ref[...], preferred_element_type=F32).astype(BF16)
        ab = jnp.concatenate([a, b], axis=0)
        pq = jnp.dot(wpq_ref[...], ab, preferred_element_type=F32)
        g0 = pl.multiple_of(i * rows_per_tile, rows_per_tile)
        p_scr[pl.ds(g0, rows_per_tile), :, :] = pq[:tile].reshape(rows_per_tile, width, cg)
        q_scr[pl.ds(g0, rows_per_tile), :, :] = pq[tile:].reshape(rows_per_tile, width, cg)
        return carry

    lax.fori_loop(0, t // tile, width_stage, 0, unroll=4)

    def row_stage(j, carry):
        w0 = pl.multiple_of(j * oct_w, oct_w)
        pj = p_scr[:, pl.ds(w0, oct_w), :].reshape(rows * oct_w, cg).astype(BF16)
        qj = q_scr[:, pl.ds(w0, oct_w), :].reshape(rows * oct_w, cg).astype(BF16)
        mixed = (jnp.dot(krc_ref[...], pj, preferred_element_type=F32)
                 - jnp.dot(krs_ref[...], qj, preferred_element_type=F32)) * scale
        f = jnp.dot(mixed.astype(BF16), w4_ref[...], preferred_element_type=F32)
        f_scr[:, pl.ds(w0, oct_w), :] = f.reshape(rows, oct_w, cg)
        return carry

    lax.fori_loop(0, width // oct_w, row_stage, 0, unroll=4)

    def gate_stage(i, carry):
        r0 = pl.multiple_of(i * tile, tile)
        g0 = pl.multiple_of(i * rows_per_tile, rows_per_tile)
        f = f_scr[pl.ds(g0, rows_per_tile), :, :].reshape(tile, cg)
        z = z_ref[pl.ds(r0, tile), :].astype(F32)
        o_ref[pl.ds(r0, tile), :] = (f * _silu(z)).astype(o_ref.dtype)
        return carry

    lax.fori_loop(0, t // tile, gate_stage, 0)


def _fourier_lat(proj, w4, groups, cg, u_blk, z_blk):
    bsz, t, _ = proj.shape
    width = GRID_W
    rows = t // width
    tile = 256
    oct_w = 8
    cc, sc = _dft_cos_sin(cg)
    cw, sw = _dft_cos_sin(width)
    cr, sr = _dft_cos_sin(rows)
    eye_t = np.eye(tile // width)
    bwc, bws = np.kron(eye_t, cw), np.kron(eye_t, sw)
    wpq = np.block([[bwc, -bws], [bws, bwc]])
    eye_o = np.eye(oct_w)
    krc, krs = np.kron(cr, eye_o), np.kron(sr, eye_o)
    consts = [jnp.asarray(m, F32).astype(BF16) for m in (cc, sc, wpq, krc, krs)]
    body = functools.partial(_four_lat_body, scale=float(1.0 / np.sqrt(t * cg)), tile=tile, oct_w=oct_w)
    full = lambda a: pl.BlockSpec(a.shape, lambda b, g: (0, 0))
    return pl.pallas_call(
        body,
        grid=(bsz, groups),
        in_specs=[pl.BlockSpec((None, t, cg), lambda b, g: (b, 0, u_blk + g)),
                  pl.BlockSpec((None, t, cg), lambda b, g: (b, 0, z_blk + g))]
                 + [full(a) for a in consts]
                 + [pl.BlockSpec((None, cg, cg), lambda b, g: (g, 0, 0))],
        out_specs=pl.BlockSpec((None, t, cg), lambda b, g: (b, 0, g)),
        out_shape=jax.ShapeDtypeStruct((bsz, t, groups * cg), BF16),
        scratch_shapes=[pltpu.VMEM((rows, width, cg), F32)] * 3,
        compiler_params=_params("arbitrary", "arbitrary"),
        name="fourier_lat",
    )(proj, proj, *consts, w4)


def _out_body(m_ref, f_ref, w_ref, x_ref, gate_ref, fw_ref, o_ref, *, tn):
    mw = m_ref.shape[1]
    d = o_ref.shape[1]
    m = m_ref[...]
    f = f_ref[...]
    ssq = jnp.zeros((o_ref.shape[0], 1), F32)
    for n in range(d // tn):
        cols = slice(n * tn, (n + 1) * tn)
        acc = (jnp.dot(m, w_ref[:mw, cols], preferred_element_type=F32)
               + jnp.dot(f, w_ref[mw:, cols], preferred_element_type=F32))
        y = x_ref[:, cols] + gate_ref[:, cols] * acc
        ssq = ssq + jnp.sum(y * y, axis=-1, keepdims=True)
        o_ref[:, cols] = y
    inv = lax.rsqrt(ssq * (1.0 / d) + EPS)
    for n in range(d // tn):
        cols = slice(n * tn, (n + 1) * tn)
        o_ref[:, cols] = o_ref[:, cols] * inv * fw_ref[:, cols]


def _out_proj(m_out, f_out, w_out, x, mod, mod_row, final_w):
    bsz, t, d = x.shape
    mw = m_out.shape[-1]
    tm = min(256, t)
    tiles = t // tm
    m2 = m_out.reshape(bsz * t, mw)
    f2 = f_out.reshape(bsz * t, f_out.shape[-1])
    x2 = x.reshape(bsz * t, d)
    y = pl.pallas_call(
        functools.partial(_out_body, tn=min(512, d)),
        grid=(bsz * tiles,),
        in_specs=[pl.BlockSpec((tm, mw), lambda i: (i, 0)),
                  pl.BlockSpec((tm, f2.shape[1]), lambda i: (i, 0)),
                  pl.BlockSpec(w_out.shape, lambda i: (0, 0), pipeline_mode=pl.Buffered(1)),
                  pl.BlockSpec((tm, d), lambda i: (i, 0)),
                  pl.BlockSpec((None, 1, d), lambda i: (mod_row(i // tiles) * 3 + 2, 0, 0)),
                  pl.BlockSpec((1, d), lambda i: (0, 0))],
        out_specs=pl.BlockSpec((tm, d), lambda i: (i, 0)),
        out_shape=jax.ShapeDtypeStruct((bsz * t, d), F32),
        compiler_params=_params("arbitrary"),
        name="out_proj",
    )(m2, f2, w_out, x2, mod, final_w)
    return y.reshape(bsz, t, d)


def _layer(x, mod, mod_row, norm_w, w_main, wg, bg, hnorm_w, w4, w_out, final_w,
           init, want_state, latent, heads, groups):
    bsz, t, d = x.shape
    mw = d // 2
    dv = mw // heads
    dqk = dv // 2
    cg = (d - mw) // groups
    h, gc, gr = _prenorm(x, mod, mod_row, norm_w, wg, bg, heads, min(MLSTM_CHUNK, t))
    k_t, proj = _in_proj(h.reshape(bsz * t, d), w_main, heads * dqk)
    proj = proj.reshape(bsz, t, -1)
    res = _mlstm(proj, k_t, gc, gr, hnorm_w, init, heads, dqk, dv, want_state)
    u_blk = 3 * mw // cg
    z_blk = (3 * mw + (d - mw)) // cg
    four = _fourier_lat if latent else _fourier_ctx
    f_out = four(proj, w4, groups, cg, u_blk, z_blk)
    y = _out_proj(res[0], f_out, w_out, x, mod, mod_row, final_w)
    return y, res[1:]


def kernel(x_prompt, x_sample, c, state_C, state_n, state_m, c_ctx, w_ada, b_ada, norm_w, w_in, b_gates, hnorm_w, w_four, w_out, final_norm_w):
    depth = w_ada.shape[0]
    assert depth == 1, "single-layer step only"
    bp, _, d = x_prompt.shape
    bs = x_sample.shape[0]
    heads = state_C.shape[3]
    groups = w_four.shape[1]
    mw = d // 2
    n_gates = 4 * heads
    gate0 = 4 * mw
    gw = heads * GATE_SLOTS
    assert 1 + bs <= MOD_ROWS and gw <= LANES

    cvec = jnp.zeros((MOD_ROWS, d), F32).at[0].set(c_ctx).at[1:1 + bs].set(c)
    mod = _modulation(cvec, w_ada[0], b_ada).reshape(MOD_ROWS * 3, 1, d)

    w_t = jnp.swapaxes(w_in, 1, 2)[0]
    w_main = _prep_w_in(w_t, gate0, n_gates, mw // 2)
    w_g = w_t[gate0:gate0 + n_gates].T.reshape(d, 4, heads).transpose(0, 2, 1)
    w_g = jnp.pad(w_g, ((0, 0), (0, 0), (0, GATE_SLOTS - 4))).reshape(d, gw)
    wg = jnp.pad(w_g, ((0, 0), (0, LANES - gw))).astype(BF16)
    b_g = jnp.pad(b_gates[0].reshape(4, heads).T, ((0, 0), (0, GATE_SLOTS - 4))).reshape(1, gw)
    bg = jnp.pad(b_g, ((0, 0), (0, LANES - gw))).astype(F32)
    w4 = w_four[0].astype(BF16)
    wo = w_out[0].astype(BF16)
    shared = (norm_w, w_main, wg, bg, hnorm_w, w4, wo, final_norm_w.reshape(1, d))

    y_prompt, (c_new, n_new, m_new) = _layer(
        x_prompt, mod, lambda b: b * 0, *shared, None, True, False, heads, groups)
    init = (state_C[:, 0], state_n[:, 0][..., None], state_m[:, 0][..., None, None])
    y_sample, _ = _layer(
        x_sample, mod, lambda b: b + 1, *shared, init, False, True, heads, groups)

    new_c = c_new[:, None]
    new_n = n_new.reshape(bp, 1, 2, heads, -1)
    new_m = m_new.reshape(bp, 1, 2, heads)
    return (y_prompt, y_sample, new_c, new_n, new_m)
```

```python
import functools

import numpy as np
import jax
import jax.numpy as jnp
from jax import lax
from jax.experimental import pallas as pl
from jax.experimental.pallas import tpu as pltpu

F32 = jnp.float32
BF16 = jnp.bfloat16

EPS = 1e-6
GRID_W = 64
MLSTM_CHUNK = 256
MLSTM_STEP_CHUNKS = 1
GATE_SLOTS = 8
LANES = 128
MOD_ROWS = 16
V7X_VMEM_BYTES = 64 * 1024 * 1024
VMEM_LIMIT = V7X_VMEM_BYTES - 4 * 1024 * 1024


def _params(*sem):
    return pltpu.CompilerParams(dimension_semantics=sem, vmem_limit_bytes=VMEM_LIMIT)


def _log_sigmoid(x):
    return jnp.minimum(x, 0.0) - jnp.log1p(jnp.exp(-jnp.abs(x)))


def _sigmoid(x):
    return 0.5 * jnp.tanh(0.5 * x) + 0.5


def _silu(x):
    return x * _sigmoid(x)


def _mod_body(c_ref, w_ref, b_ref, o_ref):
    s = _silu(c_ref[...]).astype(BF16)
    o_ref[...] = jnp.dot(s, w_ref[...].astype(BF16), preferred_element_type=F32) + b_ref[...]


def _modulation(cvec, w_ada, b_ada):
    rows, d = cvec.shape
    n = w_ada.shape[1]
    tn = min(512, n)
    return pl.pallas_call(
        _mod_body,
        grid=(n // tn,),
        in_specs=[pl.BlockSpec((rows, d), lambda j: (0, 0)),
                  pl.BlockSpec((d, tn), lambda j: (0, j)),
                  pl.BlockSpec((1, tn), lambda j: (0, j))],
        out_specs=pl.BlockSpec((rows, tn), lambda j: (0, j)),
        out_shape=jax.ShapeDtypeStruct((rows, n), F32),
        compiler_params=_params("arbitrary"),
        name="modulation",
    )(cvec, w_ada, b_ada)


def _wprep_body(w_ref, o_ref):
    o_ref[...] = w_ref[...].astype(BF16)


def _prep_w_in(w_t, gate0, n_gates, qk_rows):
    n, d = w_t.shape
    tr = 256
    assert qk_rows % tr == 0 and gate0 % tr == 0 and (n - gate0 - n_gates) % tr == 0 and n_gates % 8 == 0
    q_dst = n - n_gates - qk_rows

    def src_row(j):
        r = j * tr
        src = jnp.where(r < gate0 - qk_rows, r + qk_rows,
                        jnp.where(r < q_dst, r + qk_rows + n_gates, r - q_dst))
        return pl.multiple_of(src, 8)

    return pl.pallas_call(
        _wprep_body,
        grid=((n - n_gates) // tr,),
        in_specs=[pl.BlockSpec((pl.Element(tr), pl.Element(d)), lambda j: (src_row(j), 0))],
        out_specs=pl.BlockSpec((tr, d), lambda j: (j, 0)),
        out_shape=jax.ShapeDtypeStruct((n - n_gates, d), BF16),
        compiler_params=_params("arbitrary"),
        name="w_in_cast",
    )(w_t)


def _split3(x):
    hi = x.astype(BF16)
    r1 = x - hi.astype(F32)
    mid = r1.astype(BF16)
    lo = (r1 - mid.astype(F32)).astype(BF16)
    return hi, mid, lo


def _prenorm_body(x_ref, sh_ref, sc_ref, nw_ref, wg_ref, bg_ref, low_ref, upp_ref,
                  h_ref, gc_ref, gr_ref, inv_scr, mul_scr, add_scr, *, heads, chunk, sub_ssq, sub_norm):
    tm = x_ref.shape[0]

    for i in range(tm // sub_ssq):
        rows = slice(i * sub_ssq, (i + 1) * sub_ssq)
        ms = jnp.mean(jnp.square(x_ref[rows, :]), axis=-1, keepdims=True)
        inv_scr[rows, :] = jnp.broadcast_to(lax.rsqrt(ms + EPS), (sub_ssq, LANES))
    mul_scr[...] = jnp.broadcast_to(nw_ref[...] * (1.0 + sc_ref[...]), mul_scr.shape)
    add_scr[...] = jnp.broadcast_to(sh_ref[...], add_scr.shape)

    for i in range(tm // sub_norm):
        rows = slice(i * sub_norm, (i + 1) * sub_norm)
        inv = inv_scr[rows, :]
        for lt in range(x_ref.shape[1] // LANES):
            cols = slice(lt * LANES, (lt + 1) * LANES)
            y = x_ref[rows, cols] * inv
            h_ref[rows, cols] = (y * mul_scr[:, cols] + add_scr[:, cols]).astype(BF16)

    kh = x_ref.shape[1] // 2
    g = (jnp.dot(h_ref[:, :kh], wg_ref[:kh, :], preferred_element_type=F32)
         + jnp.dot(h_ref[:, kh:], wg_ref[kh:, :], preferred_element_type=F32)) + bg_ref[...]
    slot = lax.broadcasted_iota(jnp.int32, (chunk, LANES), 1) % GATE_SLOTS
    for c in range(tm // chunk):
        gch = g[c * chunk:(c + 1) * chunk]
        lf = jnp.where((slot == 1) | (slot == 3), _log_sigmoid(gch), 0.0)
        hi, mid, lo = _split3(lf)
        lf3 = jnp.concatenate([hi, mid, lo], axis=1)
        pre = jnp.dot(low_ref[...], lf3, preferred_element_type=F32)
        suf = jnp.dot(upp_ref[...], lf3, preferred_element_type=F32)
        pre = pre[:, :LANES] + pre[:, LANES:2 * LANES] + pre[:, 2 * LANES:]
        suf = suf[:, :LANES] + suf[:, LANES:2 * LANES] + suf[:, 2 * LANES:]
        b = jnp.where(slot == 1, pre, jnp.where(slot == 3, suf, 0.0))
        u = gch - pltpu.roll(b, LANES - 1, 1)
        res = jnp.where((slot == 0) | (slot == 2), u, b)
        for hd in range(heads):
            gc_ref[hd, c * chunk:(c + 1) * chunk, :] = res[:, hd * GATE_SLOTS:(hd + 1) * GATE_SLOTS]
        gr_ref[:, c * chunk:(c + 1) * chunk] = res.T[:heads * GATE_SLOTS, :]


def _prenorm(x, mod, mod_row, norm_w, wg, bg, heads, chunk):
    bsz, t, d = x.shape
    tm = min(512, t)
    assert tm % chunk == 0
    sub_norm = min(16, tm)
    gw = heads * GATE_SLOTS
    tri = np.tril(np.ones((chunk, chunk), np.float32))
    low, upp = jnp.asarray(tri).astype(BF16), jnp.asarray(tri.T).astype(BF16)
    body = functools.partial(_prenorm_body, heads=heads, chunk=chunk,
                             sub_ssq=min(64, tm), sub_norm=sub_norm)
    return pl.pallas_call(
        body,
        grid=(bsz, t // tm),
        in_specs=[pl.BlockSpec((None, tm, d), lambda b, i: (b, i, 0)),
                  pl.BlockSpec((None, 1, d), lambda b, i: (mod_row(b) * 3, 0, 0)),
                  pl.BlockSpec((None, 1, d), lambda b, i: (mod_row(b) * 3 + 1, 0, 0)),
                  pl.BlockSpec((1, d), lambda b, i: (0, 0)),
                  pl.BlockSpec((d, LANES), lambda b, i: (0, 0)),
                  pl.BlockSpec((1, LANES), lambda b, i: (0, 0)),
                  pl.BlockSpec((chunk, chunk), lambda b, i: (0, 0)),
                  pl.BlockSpec((chunk, chunk), lambda b, i: (0, 0))],
        out_specs=[pl.BlockSpec((None, tm, d), lambda b, i: (b, i, 0)),
                   pl.BlockSpec((None, heads, tm, GATE_SLOTS), lambda b, i: (b, 0, i, 0)),
                   pl.BlockSpec((None, gw, tm), lambda b, i: (b, 0, i))],
        out_shape=[jax.ShapeDtypeStruct((bsz, t, d), BF16),
                   jax.ShapeDtypeStruct((bsz, heads, t, GATE_SLOTS), F32),
                   jax.ShapeDtypeStruct((bsz, gw, t), F32)],
        scratch_shapes=[pltpu.VMEM((tm, LANES), F32), pltpu.VMEM((sub_norm, d), F32),
                        pltpu.VMEM((sub_norm, d), F32)],
        compiler_params=_params("arbitrary", "arbitrary"),
        name="prenorm",
    )(x, mod, mod, norm_w, wg, bg, low, upp)


def _in_proj_body(a_ref, w_ref, kt_ref, o_ref):
    j = pl.program_id(1)
    nt = (((1,), (1,)), ((), ()))

    @pl.when(j == 0)
    def _():
        kt_ref[...] = lax.dot_general(w_ref[...], a_ref[...], nt,
                                      preferred_element_type=F32).astype(kt_ref.dtype)

    @pl.when(j > 0)
    def _():
        o_ref[...] = lax.dot_general(a_ref[...], w_ref[...], nt,
                                     preferred_element_type=F32).astype(o_ref.dtype)


def _in_proj(h, w_t, k_rows):
    m, k = h.shape
    n = w_t.shape[0]
    tm, tn = min(1024, m), k_rows
    assert n % tn == 0 and tn % 256 == 0
    return pl.pallas_call(
        _in_proj_body,
        grid=(m // tm, n // tn),
        in_specs=[pl.BlockSpec((tm, k), lambda i, j: (i, 0)),
                  pl.BlockSpec((tn, k), lambda i, j: (j, 0))],
        out_specs=[pl.BlockSpec((tn, tm), lambda i, j: (0, i)),
                   pl.BlockSpec((tm, tn), lambda i, j: (i, jnp.maximum(j - 1, 0)))],
        out_shape=[jax.ShapeDtypeStruct((k_rows, m), BF16),
                   jax.ShapeDtypeStruct((m, n - k_rows), BF16)],
        compiler_params=_params("arbitrary", "arbitrary"),
        name="in_proj",
    )(h, w_t)


def _mlstm_body(*refs, nc, chunk, has_init, has_state_out, k_scale):
    (qf_ref, kf_ref, vf_ref, gcf_ref, grf_ref, qb_ref, kb_ref, vb_ref, gcb_ref, grb_ref,
     o_ref, z_ref, hw_ref, lown_ref, uppn_ref) = refs[:15]
    pos = 15
    if has_init:
        c0_ref, n0_ref, m0_ref = refs[pos:pos + 3]
        pos += 3
    out_ref = refs[pos]
    pos += 1
    if has_state_out:
        cn_ref, nn_ref, mn_ref = refs[pos:pos + 3]
        pos += 3
    c_scr, m_scr, hf_scr, hb_scr = refs[pos:]
    step = pl.program_id(2)
    blk, dqk = qf_ref.shape
    n_sub = blk // chunk
    dv = vf_ref.shape[1]

    @pl.when(step == 0)
    def _():
        for d in range(2):
            if has_init:
                c_scr[d, :, :dv] = c0_ref[d]
                c_scr[d, :, dv:] = jnp.broadcast_to(n0_ref[d], (dqk, LANES))
                m_scr[d] = m0_ref[d]
            else:
                c_scr[d] = jnp.zeros(c_scr.shape[1:], F32)
                m_scr[d] = jnp.zeros(m_scr.shape[1:], F32)

    ones_tile = jnp.ones((chunk, LANES), BF16)

    def scan_chunk(d, sub, q_ref, k_ref, v_ref, gc_ref, gr_ref):
        rows = slice(sub * chunk, (sub + 1) * chunk)
        q = q_ref[rows, :]
        kt = k_ref[:, rows] * jnp.asarray(k_scale, BF16)
        qk = jnp.dot(q, kt, preferred_element_type=F32)
        v_aug = jnp.concatenate([v_ref[rows, :], ones_tile], axis=1)
        b_col = gc_ref[rows, 2 * d + 1:2 * d + 2]
        u_row = gr_ref[2 * d:2 * d + 1, rows]
        b_all = b_col[chunk - 1:chunk] if d == 0 else b_col[0:1]
        m_old = m_scr[d]
        c_old = c_scr[d]

        um = u_row + (lown_ref if d == 0 else uppn_ref)[...]
        mt_u = jnp.maximum(m_old, jnp.max(um, axis=1, keepdims=True))
        sm = (qk * jnp.exp(um - mt_u)).astype(BF16)
        q_in = q * jnp.exp(m_old - mt_u).astype(BF16)
        nd = (jnp.dot(sm, v_aug, preferred_element_type=F32)
              + jnp.dot(q_in, c_old.astype(BF16), preferred_element_type=F32))
        den = nd[:, dv:dv + 1]
        h = nd[:, :dv] * (1.0 / jnp.maximum(jnp.abs(den), jnp.exp(-(b_col + mt_u))))

        m_new = b_all + jnp.maximum(m_old, jnp.max(u_row, axis=1, keepdims=True))
        decay = jnp.exp(b_all + m_old - m_new)
        kw_t = kt * jnp.exp(b_all + u_row - m_new).astype(BF16)
        c_scr[d] = decay * c_old + jnp.dot(kw_t, v_aug, preferred_element_type=F32)
        m_scr[d] = m_new
        return h

    rf = pl.multiple_of(step * blk, blk)
    rb = pl.multiple_of((nc - 1 - step) * blk, blk)
    for sub in range(n_sub):
        bsub = n_sub - 1 - sub
        hf_scr[pl.ds(rf + sub * chunk, chunk), :] = scan_chunk(0, sub, qf_ref, kf_ref, vf_ref, gcf_ref, grf_ref)
        hb_scr[pl.ds(rb + bsub * chunk, chunk), :] = scan_chunk(1, bsub, qb_ref, kb_ref, vb_ref, gcb_ref, grb_ref)

    def finalize(r0):
        for sub in range(n_sub):
            rows = pl.ds(r0 + sub * chunk, chunk)
            hm = hf_scr[rows, :] + hb_scr[rows, :]
            hm = hm * lax.rsqrt(jnp.mean(hm * hm, axis=-1, keepdims=True) + EPS)
            gate = _sigmoid(o_ref[rows, :])
            zg = _silu(z_ref[rows, :])
            out_ref[rows, :] = hm.astype(BF16) * hw_ref[...].astype(BF16) * gate * zg

    if nc == 1:
        finalize(0)
    else:
        @pl.when(step >= nc // 2)
        def _():
            finalize(rf)
            finalize(rb)

    if has_state_out:
        @pl.when(step == nc - 1)
        def _():
            for d in range(2):
                cn_ref[d] = c_scr[d, :, :dv]
                nn_ref[d] = c_scr[d, :, dv:dv + 1]
                mn_ref[d] = m_scr[d]


def _mlstm(proj, k_t, gc, gr, hnorm_w, init, heads, dqk, dv, want_state):
    bsz, t, n_proj = proj.shape
    chunk = min(MLSTM_CHUNK, t)
    blk = min(MLSTM_STEP_CHUNKS * chunk, t)
    nc = t // blk
    assert nc == 1 or nc % 2 == 0
    q_blk = (n_proj - heads * dqk) // dqk
    tri = np.tril(np.ones((chunk, chunk), bool))
    low = jnp.asarray(np.where(tri, 0.0, -np.inf).astype(np.float32))
    upp = jnp.asarray(np.where(tri.T, 0.0, -np.inf).astype(np.float32))

    def scan_specs(ci):
        return [
            pl.BlockSpec((None, blk, dqk), lambda b, h, s: (b, ci(s), q_blk + h)),
            pl.BlockSpec((dqk, blk), lambda b, h, s: (h, b * nc + ci(s))),
            pl.BlockSpec((None, blk, dv), lambda b, h, s: (b, ci(s), h)),
            pl.BlockSpec((None, None, blk, GATE_SLOTS), lambda b, h, s: (b, h, ci(s), 0)),
            pl.BlockSpec((None, GATE_SLOTS, blk), lambda b, h, s: (b, h, ci(s))),
        ]

    in_specs = scan_specs(lambda s: s) + scan_specs(lambda s: nc - 1 - s) + [
        pl.BlockSpec((None, t, dv), lambda b, h, s: (b, 0, heads + h)),
        pl.BlockSpec((None, t, dv), lambda b, h, s: (b, 0, 2 * heads + h)),
        pl.BlockSpec((1, dv), lambda b, h, s: (0, h)),
        pl.BlockSpec((chunk, chunk), lambda b, h, s: (0, 0)),
        pl.BlockSpec((chunk, chunk), lambda b, h, s: (0, 0)),
    ]
    scan_args = [proj, k_t, proj, gc, gr]
    args = scan_args + scan_args + [proj, proj, hnorm_w, low, upp]
    state_specs = [
        pl.BlockSpec((None, 2, None, dqk, dv), lambda b, h, s: (b, 0, h, 0, 0)),
        pl.BlockSpec((None, 2, None, dqk, 1), lambda b, h, s: (b, 0, h, 0, 0)),
        pl.BlockSpec((None, 2, None, 1, 1), lambda b, h, s: (b, 0, h, 0, 0)),
    ]
    if init is not None:
        in_specs += state_specs
        args += list(init)
    out_specs = [pl.BlockSpec((None, t, dv), lambda b, h, s: (b, 0, h))]
    out_shape = [jax.ShapeDtypeStruct((bsz, t, heads * dv), BF16)]
    if want_state:
        out_specs += state_specs
        out_shape += [
            jax.ShapeDtypeStruct((bsz, 2, heads, dqk, dv), F32),
            jax.ShapeDtypeStruct((bsz, 2, heads, dqk, 1), F32),
            jax.ShapeDtypeStruct((bsz, 2, heads, 1, 1), F32),
        ]
    body = functools.partial(_mlstm_body, nc=nc, chunk=chunk, has_init=init is not None,
                             has_state_out=want_state, k_scale=dqk ** -0.5)
    return pl.pallas_call(
        body,
        grid=(bsz, heads, nc),
        in_specs=in_specs,
        out_specs=out_specs,
        out_shape=out_shape,
        scratch_shapes=[pltpu.VMEM((2, dqk, dv + LANES), F32), pltpu.VMEM((2, 1, 1), F32),
                        pltpu.VMEM((t, dv), F32), pltpu.VMEM((t, dv), F32)],
        compiler_params=_params("arbitrary", "arbitrary", "arbitrary"),
        name="mlstm",
    )(*args)


def _dft_cos_sin(n):
    ang = 2.0 * np.pi * np.outer(np.arange(n), np.arange(n)) / n
    return np.cos(ang), np.sin(ang)


def _four_ctx_body(u_ref, z_ref, cc_ref, sc_ref, ct_ref, st_ref, w4_ref, o_ref, *, scale):
    cg = cc_ref.shape[0]
    for g in range(w4_ref.shape[0]):
        cols = slice(g * cg, (g + 1) * cg)
        x = u_ref[:, cols]
        a = jnp.dot(x, cc_ref[...], preferred_element_type=F32).astype(BF16)
        b = jnp.dot(x, sc_ref[...], preferred_element_type=F32).astype(BF16)
        mixed = (jnp.dot(ct_ref[...], a, preferred_element_type=F32)
                 - jnp.dot(st_ref[...], b, preferred_element_type=F32)) * scale
        f = jnp.dot(mixed.astype(BF16), w4_ref[g], preferred_element_type=F32)
        o_ref[:, cols] = (f * _silu(z_ref[:, cols].astype(F32))).astype(o_ref.dtype)


def _fourier_ctx(proj, w4, groups, cg, u_blk, z_blk):
    bsz, t, _ = proj.shape
    fw = groups * cg
    assert (u_blk * cg) % fw == 0 and (z_blk * cg) % fw == 0
    cc, sc = _dft_cos_sin(cg)
    ct, st = _dft_cos_sin(t)
    consts = [jnp.asarray(m, F32).astype(BF16) for m in (cc, sc, ct, st)]
    body = functools.partial(_four_ctx_body, scale=float(1.0 / np.sqrt(t * cg)))
    sq = lambda n: pl.BlockSpec((n, n), lambda b: (0, 0))
    return pl.pallas_call(
        body,
        grid=(bsz,),
        in_specs=[pl.BlockSpec((None, t, fw), lambda b: (b, 0, u_blk * cg // fw)),
                  pl.BlockSpec((None, t, fw), lambda b: (b, 0, z_blk * cg // fw)),
                  sq(cg), sq(cg), sq(t), sq(t),
                  pl.BlockSpec((groups, cg, cg), lambda b: (0, 0, 0))],
        out_specs=pl.BlockSpec((None, t, fw), lambda b: (b, 0, 0)),
        out_shape=jax.ShapeDtypeStruct((bsz, t, fw), BF16),
        compiler_params=_params("arbitrary"),
        name="fourier_ctx",
    )(proj, proj, *consts, w4)


def _four_lat_body(u_ref, z_ref, cc_ref, sc_ref, wpq_ref, krc_ref, krs_ref, w4_ref, o_ref,
                   p_scr, q_scr, f_scr, *, scale, tile, oct_w):
    rows, width, cg = p_scr.shape
    t = rows * width
    rows_per_tile = tile // width

    def width_stage(i, carry):
        r0 = pl.multiple_of(i * tile, tile)
        x = u_ref[pl.ds(r0, tile), :]
        a = jnp.dot(x, cc_ref[...], preferred_element_type=F32).astype(BF16)
        b = jnp.dot(x, sc_ref[...], preferred_element_type=F32).astype(BF16)
        ab = jnp.concatenate([a, b], axis=0)
        pq = jnp.dot(wpq_ref[...], ab, preferred_element_type=F32)
        g0 = pl.multiple_of(i * rows_per_tile, rows_per_tile)
        p_scr[pl.ds(g0, rows_per_tile), :, :] = pq[:tile].reshape(rows_per_tile, width, cg)
        q_scr[pl.ds(g0, rows_per_tile), :, :] = pq[tile:].reshape(rows_per_tile, width, cg)
        return carry

    lax.fori_loop(0, t // tile, width_stage, 0, unroll=4)

    def row_stage(j, carry):
        w0 = pl.multiple_of(j * oct_w, oct_w)
        pj = p_scr[:, pl.ds(w0, oct_w), :].reshape(rows * oct_w, cg).astype(BF16)
        qj = q_scr[:, pl.ds(w0, oct_w), :].reshape(rows * oct_w, cg).astype(BF16)
        mixed = (jnp.dot(krc_ref[...], pj, preferred_element_type=F32)
                 - jnp.dot(krs_ref[...], qj, preferred_element_type=F32)) * scale
        f = jnp.dot(mixed.astype(BF16), w4_ref[...], preferred_element_type=F32)
        f_scr[:, pl.ds(w0, oct_w), :] = f.reshape(rows, oct_w, cg)
        return carry

    lax.fori_loop(0, width // oct_w, row_stage, 0, unroll=4)

    def gate_stage(i, carry):
        r0 = pl.multiple_of(i * tile, tile)
        g0 = pl.multiple_of(i * rows_per_tile, rows_per_tile)
        f = f_scr[pl.ds(g0, rows_per_tile), :, :].reshape(tile, cg)
        z = z_ref[pl.ds(r0, tile), :].astype(F32)
        o_ref[pl.ds(r0, tile), :] = (f * _silu(z)).astype(o_ref.dtype)
        return carry

    lax.fori_loop(0, t // tile, gate_stage, 0)


def _fourier_lat(proj, w4, groups, cg, u_blk, z_blk):
    bsz, t, _ = proj.shape
    width = GRID_W
    rows = t // width
    tile = 256
    oct_w = 8
    cc, sc = _dft_cos_sin(cg)
    cw, sw = _dft_cos_sin(width)
    cr, sr = _dft_cos_sin(rows)
    eye_t = np.eye(tile // width)
    bwc, bws = np.kron(eye_t, cw), np.kron(eye_t, sw)
    wpq = np.block([[bwc, -bws], [bws, bwc]])
    eye_o = np.eye(oct_w)
    krc, krs = np.kron(cr, eye_o), np.kron(sr, eye_o)
    consts = [jnp.asarray(m, F32).astype(BF16) for m in (cc, sc, wpq, krc, krs)]
    body = functools.partial(_four_lat_body, scale=float(1.0 / np.sqrt(t * cg)), tile=tile, oct_w=oct_w)
    full = lambda a: pl.BlockSpec(a.shape, lambda b, g: (0, 0))
    return pl.pallas_call(
        body,
        grid=(bsz, groups),
        in_specs=[pl.BlockSpec((None, t, cg), lambda b, g: (b, 0, u_blk + g)),
                  pl.BlockSpec((None, t, cg), lambda b, g: (b, 0, z_blk + g))]
                 + [full(a) for a in consts]
                 + [pl.BlockSpec((None, cg, cg), lambda b, g: (g, 0, 0))],
        out_specs=pl.BlockSpec((None, t, cg), lambda b, g: (b, 0, g)),
        out_shape=jax.ShapeDtypeStruct((bsz, t, groups * cg), BF16),
        scratch_shapes=[pltpu.VMEM((rows, width, cg), F32)] * 3,
        compiler_params=_params("arbitrary", "arbitrary"),
        name="fourier_lat",
    )(proj, proj, *consts, w4)


def _out_body(m_ref, f_ref, w_ref, x_ref, gate_ref, fw_ref, o_ref, *, tn):
    mw = m_ref.shape[1]
    d = o_ref.shape[1]
    m = m_ref[...]
    f = f_ref[...]
    ssq = jnp.zeros((o_ref.shape[0], 1), F32)
    for n in range(d // tn):
        cols = slice(n * tn, (n + 1) * tn)
        acc = (jnp.dot(m, w_ref[:mw, cols], preferred_element_type=F32)
               + jnp.dot(f, w_ref[mw:, cols], preferred_element_type=F32))
        y = x_ref[:, cols] + gate_ref[:, cols] * acc
        ssq = ssq + jnp.sum(y * y, axis=-1, keepdims=True)
        o_ref[:, cols] = y
    inv = lax.rsqrt(ssq * (1.0 / d) + EPS)
    for n in range(d // tn):
        cols = slice(n * tn, (n + 1) * tn)
        o_ref[:, cols] = o_ref[:, cols] * inv * fw_ref[:, cols]


def _out_proj(m_out, f_out, w_out, x, mod, mod_row, final_w):
    bsz, t, d = x.shape
    mw = m_out.shape[-1]
    tm = min(256, t)
    tiles = t // tm
    m2 = m_out.reshape(bsz * t, mw)
    f2 = f_out.reshape(bsz * t, f_out.shape[-1])
    x2 = x.reshape(bsz * t, d)
    y = pl.pallas_call(
        functools.partial(_out_body, tn=min(512, d)),
        grid=(bsz * tiles,),
        in_specs=[pl.BlockSpec((tm, mw), lambda i: (i, 0)),
                  pl.BlockSpec((tm, f2.shape[1]), lambda i: (i, 0)),
                  pl.BlockSpec(w_out.shape, lambda i: (0, 0), pipeline_mode=pl.Buffered(1)),
                  pl.BlockSpec((tm, d), lambda i: (i, 0)),
                  pl.BlockSpec((None, 1, d), lambda i: (mod_row(i // tiles) * 3 + 2, 0, 0)),
                  pl.BlockSpec((1, d), lambda i: (0, 0))],
        out_specs=pl.BlockSpec((tm, d), lambda i: (i, 0)),
        out_shape=jax.ShapeDtypeStruct((bsz * t, d), F32),
        compiler_params=_params("arbitrary"),
        name="out_proj",
    )(m2, f2, w_out, x2, mod, final_w)
    return y.reshape(bsz, t, d)


def _layer(x, mod, mod_row, norm_w, w_main, wg, bg, hnorm_w, w4, w_out, final_w,
           init, want_state, latent, heads, groups):
    bsz, t, d = x.shape
    mw = d // 2
    dv = mw // heads
    dqk = dv // 2
    cg = (d - mw) // groups
    h, gc, gr = _prenorm(x, mod, mod_row, norm_w, wg, bg, heads, min(MLSTM_CHUNK, t))
    k_t, proj = _in_proj(h.reshape(bsz * t, d), w_main, heads * dqk)
    proj = proj.reshape(bsz, t, -1)
    res = _mlstm(proj, k_t, gc, gr, hnorm_w, init, heads, dqk, dv, want_state)
    u_blk = 3 * mw // cg
    z_blk = (3 * mw + (d - mw)) // cg
    four = _fourier_lat if latent else _fourier_ctx
    f_out = four(proj, w4, groups, cg, u_blk, z_blk)
    y = _out_proj(res[0], f_out, w_out, x, mod, mod_row, final_w)
    return y, res[1:]


def kernel(x_prompt, x_sample, c, state_C, state_n, state_m, c_ctx, w_ada, b_ada, norm_w, w_in, b_gates, hnorm_w, w_four, w_out, final_norm_w):
    depth = w_ada.shape[0]
    assert depth == 1, "single-layer step only"
    bp, _, d = x_prompt.shape
    bs = x_sample.shape[0]
    heads = state_C.shape[3]
    groups = w_four.shape[1]
    mw = d // 2
    n_gates = 4 * heads
    gate0 = 4 * mw
    gw = heads * GATE_SLOTS
    assert 1 + bs <= MOD_ROWS and gw <= LANES

    cvec = jnp.zeros((MOD_ROWS, d), F32).at[0].set(c_ctx).at[1:1 + bs].set(c)
    mod = _modulation(cvec, w_ada[0], b_ada).reshape(MOD_ROWS * 3, 1, d)

    w_t = jnp.swapaxes(w_in, 1, 2)[0]
    w_main = _prep_w_in(w_t, gate0, n_gates, mw // 2)
    w_g = w_t[gate0:gate0 + n_gates].T.reshape(d, 4, heads).transpose(0, 2, 1)
    w_g = jnp.pad(w_g, ((0, 0), (0, 0), (0, GATE_SLOTS - 4))).reshape(d, gw)
    wg = jnp.pad(w_g, ((0, 0), (0, LANES - gw))).astype(BF16)
    b_g = jnp.pad(b_gates[0].reshape(4, heads).T, ((0, 0), (0, GATE_SLOTS - 4))).reshape(1, gw)
    bg = jnp.pad(b_g, ((0, 0), (0, LANES - gw))).astype(F32)
    w4 = w_four[0].astype(BF16)
    wo = w_out[0].astype(BF16)
    shared = (norm_w, w_main, wg, bg, hnorm_w, w4, wo, final_norm_w.reshape(1, d))

    y_prompt, (c_new, n_new, m_new) = _layer(
        x_prompt, mod, lambda b: b * 0, *shared, None, True, False, heads, groups)
    init = (state_C[:, 0], state_n[:, 0][..., None], state_m[:, 0][..., None, None])
    y_sample, _ = _layer(
        x_sample, mod, lambda b: b + 1, *shared, init, False, True, heads, groups)

    new_c = c_new[:, None]
    new_n = n_new.reshape(bp, 1, 2, heads, -1)
    new_m = m_new.reshape(bp, 1, 2, heads)
    return (y_prompt, y_sample, new_c, new_n, new_m)
```

```python
import functools

import numpy as np
import jax
import jax.numpy as jnp
from jax import lax
from jax.experimental import pallas as pl
from jax.experimental.pallas import tpu as pltpu

F32 = jnp.float32
BF16 = jnp.bfloat16

EPS = 1e-6
GRID_W = 64
MLSTM_CHUNK = 256
GATE_SLOTS = 8
LANES = 128
MOD_ROWS = 16
V7X_VMEM_BYTES = 64 * 1024 * 1024
VMEM_LIMIT = V7X_VMEM_BYTES - 4 * 1024 * 1024


def _params(*sem):
    return pltpu.CompilerParams(dimension_semantics=sem, vmem_limit_bytes=VMEM_LIMIT)


def _log_sigmoid(x):
    return jnp.minimum(x, 0.0) - jnp.log1p(jnp.exp(-jnp.abs(x)))


def _sigmoid(x):
    return 0.5 * jnp.tanh(0.5 * x) + 0.5


def _silu(x):
    return x * _sigmoid(x)


def _mod_body(c_ref, w_ref, b_ref, o_ref):
    s = _silu(c_ref[...]).astype(BF16)
    o_ref[...] = jnp.dot(s, w_ref[...].astype(BF16), preferred_element_type=F32) + b_ref[...]


def _modulation(cvec, w_ada, b_ada):
    rows, d = cvec.shape
    n = w_ada.shape[1]
    tn = min(512, n)
    return pl.pallas_call(
        _mod_body,
        grid=(n // tn,),
        in_specs=[pl.BlockSpec((rows, d), lambda j: (0, 0)),
                  pl.BlockSpec((d, tn), lambda j: (0, j)),
                  pl.BlockSpec((1, tn), lambda j: (0, j))],
        out_specs=pl.BlockSpec((rows, tn), lambda j: (0, j)),
        out_shape=jax.ShapeDtypeStruct((rows, n), F32),
        compiler_params=_params("arbitrary"),
        name="modulation",
    )(cvec, w_ada, b_ada)


def _wprep_body(w_ref, o_ref):
    o_ref[...] = w_ref[...].astype(BF16)


def _prep_w_in(w_t, gate0, n_gates, qk_rows):
    n, d = w_t.shape
    tr = 256
    assert qk_rows % tr == 0 and gate0 % tr == 0 and (n - gate0 - n_gates) % tr == 0 and n_gates % 8 == 0
    q_dst = n - n_gates - qk_rows

    def src_row(j):
        r = j * tr
        src = jnp.where(r < gate0 - qk_rows, r + qk_rows,
                        jnp.where(r < q_dst, r + qk_rows + n_gates, r - q_dst))
        return pl.multiple_of(src, 8)

    return pl.pallas_call(
        _wprep_body,
        grid=((n - n_gates) // tr,),
        in_specs=[pl.BlockSpec((pl.Element(tr), pl.Element(d)), lambda j: (src_row(j), 0))],
        out_specs=pl.BlockSpec((tr, d), lambda j: (j, 0)),
        out_shape=jax.ShapeDtypeStruct((n - n_gates, d), BF16),
        compiler_params=_params("arbitrary"),
        name="w_in_cast",
    )(w_t)


def _split3(x):
    hi = x.astype(BF16)
    r1 = x - hi.astype(F32)
    mid = r1.astype(BF16)
    lo = (r1 - mid.astype(F32)).astype(BF16)
    return hi, mid, lo


def _prenorm_body(x_ref, sh_ref, sc_ref, nw_ref, wg_ref, bg_ref, low_ref, upp_ref,
                  h_ref, gc_ref, gr_ref, inv_scr, mul_scr, add_scr, *, heads, chunk, sub_ssq, sub_norm):
    tm = x_ref.shape[0]

    for i in range(tm // sub_ssq):
        rows = slice(i * sub_ssq, (i + 1) * sub_ssq)
        ms = jnp.mean(jnp.square(x_ref[rows, :]), axis=-1, keepdims=True)
        inv_scr[rows, :] = jnp.broadcast_to(lax.rsqrt(ms + EPS), (sub_ssq, LANES))
    mul_scr[...] = jnp.broadcast_to(nw_ref[...] * (1.0 + sc_ref[...]), mul_scr.shape)
    add_scr[...] = jnp.broadcast_to(sh_ref[...], add_scr.shape)

    for i in range(tm // sub_norm):
        rows = slice(i * sub_norm, (i + 1) * sub_norm)
        inv = inv_scr[rows, :]
        for lt in range(x_ref.shape[1] // LANES):
            cols = slice(lt * LANES, (lt + 1) * LANES)
            y = x_ref[rows, cols] * inv
            h_ref[rows, cols] = (y * mul_scr[:, cols] + add_scr[:, cols]).astype(BF16)

    kh = x_ref.shape[1] // 2
    g = (jnp.dot(h_ref[:, :kh], wg_ref[:kh, :], preferred_element_type=F32)
         + jnp.dot(h_ref[:, kh:], wg_ref[kh:, :], preferred_element_type=F32)) + bg_ref[...]
    slot = lax.broadcasted_iota(jnp.int32, (chunk, LANES), 1) % GATE_SLOTS
    for c in range(tm // chunk):
        gch = g[c * chunk:(c + 1) * chunk]
        lf = jnp.where((slot == 1) | (slot == 3), _log_sigmoid(gch), 0.0)
        hi, mid, lo = _split3(lf)
        lf3 = jnp.concatenate([hi, mid, lo], axis=1)
        pre = jnp.dot(low_ref[...], lf3, preferred_element_type=F32)
        suf = jnp.dot(upp_ref[...], lf3, preferred_element_type=F32)
        pre = pre[:, :LANES] + pre[:, LANES:2 * LANES] + pre[:, 2 * LANES:]
        suf = suf[:, :LANES] + suf[:, LANES:2 * LANES] + suf[:, 2 * LANES:]
        b = jnp.where(slot == 1, pre, jnp.where(slot == 3, suf, 0.0))
        u = gch - pltpu.roll(b, LANES - 1, 1)
        res = jnp.where((slot == 0) | (slot == 2), u, b)
        for hd in range(heads):
            gc_ref[hd, c * chunk:(c + 1) * chunk, :] = res[:, hd * GATE_SLOTS:(hd + 1) * GATE_SLOTS]
        gr_ref[:, c * chunk:(c + 1) * chunk] = res.T[:heads * GATE_SLOTS, :]


def _prenorm(x, mod, mod_row, norm_w, wg, bg, heads, chunk):
    bsz, t, d = x.shape
    tm = min(512, t)
    assert tm % chunk == 0
    sub_norm = min(16, tm)
    gw = heads * GATE_SLOTS
    tri = np.tril(np.ones((chunk, chunk), np.float32))
    low, upp = jnp.asarray(tri).astype(BF16), jnp.asarray(tri.T).astype(BF16)
    body = functools.partial(_prenorm_body, heads=heads, chunk=chunk,
                             sub_ssq=min(64, tm), sub_norm=sub_norm)
    return pl.pallas_call(
        body,
        grid=(bsz, t // tm),
        in_specs=[pl.BlockSpec((None, tm, d), lambda b, i: (b, i, 0)),
                  pl.BlockSpec((None, 1, d), lambda b, i: (mod_row(b) * 3, 0, 0)),
                  pl.BlockSpec((None, 1, d), lambda b, i: (mod_row(b) * 3 + 1, 0, 0)),
                  pl.BlockSpec((1, d), lambda b, i: (0, 0)),
                  pl.BlockSpec((d, LANES), lambda b, i: (0, 0)),
                  pl.BlockSpec((1, LANES), lambda b, i: (0, 0)),
                  pl.BlockSpec((chunk, chunk), lambda b, i: (0, 0)),
                  pl.BlockSpec((chunk, chunk), lambda b, i: (0, 0))],
        out_specs=[pl.BlockSpec((None, tm, d), lambda b, i: (b, i, 0)),
                   pl.BlockSpec((None, heads, tm, GATE_SLOTS), lambda b, i: (b, 0, i, 0)),
                   pl.BlockSpec((None, gw, tm), lambda b, i: (b, 0, i))],
        out_shape=[jax.ShapeDtypeStruct((bsz, t, d), BF16),
                   jax.ShapeDtypeStruct((bsz, heads, t, GATE_SLOTS), F32),
                   jax.ShapeDtypeStruct((bsz, gw, t), F32)],
        scratch_shapes=[pltpu.VMEM((tm, LANES), F32), pltpu.VMEM((sub_norm, d), F32),
                        pltpu.VMEM((sub_norm, d), F32)],
        compiler_params=_params("arbitrary", "arbitrary"),
        name="prenorm",
    )(x, mod, mod, norm_w, wg, bg, low, upp)


def _in_proj_body(a_ref, w_ref, kt_ref, o_ref):
    j = pl.program_id(1)
    nt = (((1,), (1,)), ((), ()))

    @pl.when(j == 0)
    def _():
        kt_ref[...] = lax.dot_general(w_ref[...], a_ref[...], nt,
                                      preferred_element_type=F32).astype(kt_ref.dtype)

    @pl.when(j > 0)
    def _():
        o_ref[...] = lax.dot_general(a_ref[...], w_ref[...], nt,
                                     preferred_element_type=F32).astype(o_ref.dtype)


def _in_proj(h, w_t, k_rows):
    m, k = h.shape
    n = w_t.shape[0]
    tm, tn = min(1024, m), k_rows
    assert n % tn == 0 and tn % 256 == 0
    return pl.pallas_call(
        _in_proj_body,
        grid=(m // tm, n // tn),
        in_specs=[pl.BlockSpec((tm, k), lambda i, j: (i, 0)),
                  pl.BlockSpec((tn, k), lambda i, j: (j, 0))],
        out_specs=[pl.BlockSpec((tn, tm), lambda i, j: (0, i)),
                   pl.BlockSpec((tm, tn), lambda i, j: (i, jnp.maximum(j - 1, 0)))],
        out_shape=[jax.ShapeDtypeStruct((k_rows, m), BF16),
                   jax.ShapeDtypeStruct((m, n - k_rows), BF16)],
        compiler_params=_params("arbitrary", "arbitrary"),
        name="in_proj",
    )(h, w_t)


def _mlstm_body(*refs, nc, has_init, has_state_out, k_scale):
    (qf_ref, kf_ref, vf_ref, gcf_ref, grf_ref, qb_ref, kb_ref, vb_ref, gcb_ref, grb_ref,
     o_ref, z_ref, hw_ref, lown_ref, uppn_ref) = refs[:15]
    pos = 15
    if has_init:
        c0_ref, n0_ref, m0_ref = refs[pos:pos + 3]
        pos += 3
    out_ref = refs[pos]
    pos += 1
    if has_state_out:
        cn_ref, nn_ref, mn_ref = refs[pos:pos + 3]
        pos += 3
    c_scr, m_scr, hf_scr, hb_scr = refs[pos:]
    step = pl.program_id(2)
    chunk = qf_ref.shape[0]
    hps = c_scr.shape[1]
    dqk = qf_ref.shape[1] // hps
    dv = vf_ref.shape[1] // hps

    @pl.when(step == 0)
    def _():
        for d in range(2):
            for hh in range(hps):
                if has_init:
                    c_scr[d, hh, :, :dv] = c0_ref[d, hh]
                    c_scr[d, hh, :, dv:] = jnp.broadcast_to(n0_ref[d, hh], (dqk, LANES))
                    m_scr[d, hh] = m0_ref[d, hh]
                else:
                    c_scr[d, hh] = jnp.zeros(c_scr.shape[2:], F32)
                    m_scr[d, hh] = jnp.zeros(m_scr.shape[2:], F32)

    ones_tile = jnp.ones((chunk, LANES), BF16)

    def scan_chunk(d, hh, q_ref, k_ref, v_ref, gc_ref, gr_ref):
        q = q_ref[:, hh * dqk:(hh + 1) * dqk]
        kt = k_ref[hh * dqk:(hh + 1) * dqk, :] * jnp.asarray(k_scale, BF16)
        qk = jnp.dot(q, kt, preferred_element_type=F32)
        v_aug = jnp.concatenate([v_ref[:, hh * dv:(hh + 1) * dv], ones_tile], axis=1)
        b_col = gc_ref[hh, :, 2 * d + 1:2 * d + 2]
        u_row = gr_ref[hh * GATE_SLOTS + 2 * d:hh * GATE_SLOTS + 2 * d + 1, :]
        b_all = b_col[chunk - 1:chunk] if d == 0 else b_col[0:1]
        m_old = m_scr[d, hh]
        c_old = c_scr[d, hh]

        um = u_row + (lown_ref if d == 0 else uppn_ref)[...]
        mt_u = jnp.maximum(m_old, jnp.max(um, axis=1, keepdims=True))
        sm = (qk * jnp.exp(um - mt_u)).astype(BF16)
        q_in = q * jnp.exp(m_old - mt_u).astype(BF16)
        nd = (jnp.dot(sm, v_aug, preferred_element_type=F32)
              + jnp.dot(q_in, c_old.astype(BF16), preferred_element_type=F32))
        den = nd[:, dv:dv + 1]
        h = nd[:, :dv] * (1.0 / jnp.maximum(jnp.abs(den), jnp.exp(-(b_col + mt_u))))

        m_new = b_all + jnp.maximum(m_old, jnp.max(u_row, axis=1, keepdims=True))
        decay = jnp.exp(b_all + m_old - m_new)
        kw_t = kt * jnp.exp(b_all + u_row - m_new).astype(BF16)
        c_scr[d, hh] = decay * c_old + jnp.dot(kw_t, v_aug, preferred_element_type=F32)
        m_scr[d, hh] = m_new
        return h

    rf = pl.multiple_of(step * chunk, chunk)
    rb = pl.multiple_of((nc - 1 - step) * chunk, chunk)
    for hh in range(hps):
        cols = slice(hh * dv, (hh + 1) * dv)
        hf_scr[pl.ds(rf, chunk), cols] = scan_chunk(0, hh, qf_ref, kf_ref, vf_ref, gcf_ref, grf_ref)
        hb_scr[pl.ds(rb, chunk), cols] = scan_chunk(1, hh, qb_ref, kb_ref, vb_ref, gcb_ref, grb_ref)

    def finalize(r0):
        rows = pl.ds(r0, chunk)
        for hh in range(hps):
            cols = slice(hh * dv, (hh + 1) * dv)
            hm = hf_scr[rows, cols] + hb_scr[rows, cols]
            hm = hm * lax.rsqrt(jnp.mean(hm * hm, axis=-1, keepdims=True) + EPS)
            gate = _sigmoid(o_ref[rows, cols])
            zg = _silu(z_ref[rows, cols])
            out_ref[rows, cols] = hm.astype(BF16) * hw_ref[:, cols].astype(BF16) * gate * zg

    if nc == 1:
        finalize(0)
    else:
        @pl.when(step >= nc // 2)
        def _():
            finalize(rf)
            finalize(rb)

    if has_state_out:
        @pl.when(step == nc - 1)
        def _():
            for d in range(2):
                for hh in range(hps):
                    cn_ref[d, hh] = c_scr[d, hh, :, :dv]
                    nn_ref[d, hh] = c_scr[d, hh, :, dv:dv + 1]
                    mn_ref[d, hh] = m_scr[d, hh]


def _mlstm(proj, k_t, gc, gr, hnorm_w, init, heads, dqk, dv, want_state):
    bsz, t, n_proj = proj.shape
    chunk = min(MLSTM_CHUNK, t)
    nc = t // chunk
    assert nc == 1 or nc % 2 == 0
    hps = heads if nc == 1 else 1
    q_blk = (n_proj - heads * dqk) // (hps * dqk)
    tri = np.tril(np.ones((chunk, chunk), bool))
    low = jnp.asarray(np.where(tri, 0.0, -np.inf).astype(np.float32))
    upp = jnp.asarray(np.where(tri.T, 0.0, -np.inf).astype(np.float32))

    def scan_specs(ci):
        return [
            pl.BlockSpec((None, chunk, hps * dqk), lambda b, h, s: (b, ci(s), q_blk + h)),
            pl.BlockSpec((hps * dqk, chunk), lambda b, h, s: (h, b * nc + ci(s))),
            pl.BlockSpec((None, chunk, hps * dv), lambda b, h, s: (b, ci(s), h)),
            pl.BlockSpec((None, hps, chunk, GATE_SLOTS), lambda b, h, s: (b, h, ci(s), 0)),
            pl.BlockSpec((None, hps * GATE_SLOTS, chunk), lambda b, h, s: (b, h, ci(s))),
        ]

    n_hg = heads // hps
    in_specs = scan_specs(lambda s: s) + scan_specs(lambda s: nc - 1 - s) + [
        pl.BlockSpec((None, t, hps * dv), lambda b, h, s: (b, 0, n_hg + h)),
        pl.BlockSpec((None, t, hps * dv), lambda b, h, s: (b, 0, 2 * n_hg + h)),
        pl.BlockSpec((1, hps * dv), lambda b, h, s: (0, h)),
        pl.BlockSpec((chunk, chunk), lambda b, h, s: (0, 0)),
        pl.BlockSpec((chunk, chunk), lambda b, h, s: (0, 0)),
    ]
    scan_args = [proj, k_t, proj, gc, gr]
    args = scan_args + scan_args + [proj, proj, hnorm_w, low, upp]
    state_specs = [
        pl.BlockSpec((None, 2, hps, dqk, dv), lambda b, h, s: (b, 0, h, 0, 0)),
        pl.BlockSpec((None, 2, hps, dqk, 1), lambda b, h, s: (b, 0, h, 0, 0)),
        pl.BlockSpec((None, 2, hps, 1, 1), lambda b, h, s: (b, 0, h, 0, 0)),
    ]
    if init is not None:
        in_specs += state_specs
        args += list(init)
    out_specs = [pl.BlockSpec((None, t, hps * dv), lambda b, h, s: (b, 0, h))]
    out_shape = [jax.ShapeDtypeStruct((bsz, t, heads * dv), BF16)]
    if want_state:
        out_specs += state_specs
        out_shape += [
            jax.ShapeDtypeStruct((bsz, 2, heads, dqk, dv), F32),
            jax.ShapeDtypeStruct((bsz, 2, heads, dqk, 1), F32),
            jax.ShapeDtypeStruct((bsz, 2, heads, 1, 1), F32),
        ]
    body = functools.partial(_mlstm_body, nc=nc, has_init=init is not None,
                             has_state_out=want_state, k_scale=dqk ** -0.5)
    return pl.pallas_call(
        body,
        grid=(bsz, n_hg, nc),
        in_specs=in_specs,
        out_specs=out_specs,
        out_shape=out_shape,
        scratch_shapes=[pltpu.VMEM((2, hps, dqk, dv + LANES), F32), pltpu.VMEM((2, hps, 1, 1), F32),
                        pltpu.VMEM((t, hps * dv), F32), pltpu.VMEM((t, hps * dv), F32)],
        compiler_params=_params("arbitrary", "arbitrary", "arbitrary"),
        name="mlstm",
    )(*args)


def _dft_cos_sin(n):
    ang = 2.0 * np.pi * np.outer(np.arange(n), np.arange(n)) / n
    return np.cos(ang), np.sin(ang)


def _four_ctx_body(u_ref, z_ref, cc_ref, sc_ref, ct_ref, st_ref, w4_ref, o_ref, *, scale):
    cg = cc_ref.shape[0]
    for g in range(w4_ref.shape[0]):
        cols = slice(g * cg, (g + 1) * cg)
        x = u_ref[:, cols]
        a = jnp.dot(x, cc_ref[...], preferred_element_type=F32).astype(BF16)
        b = jnp.dot(x, sc_ref[...], preferred_element_type=F32).astype(BF16)
        mixed = (jnp.dot(ct_ref[...], a, preferred_element_type=F32)
                 - jnp.dot(st_ref[...], b, preferred_element_type=F32)) * scale
        f = jnp.dot(mixed.astype(BF16), w4_ref[g], preferred_element_type=F32)
        o_ref[:, cols] = (f * _silu(z_ref[:, cols].astype(F32))).astype(o_ref.dtype)


def _fourier_ctx(proj, w4, groups, cg, u_blk, z_blk):
    bsz, t, _ = proj.shape
    fw = groups * cg
    assert (u_blk * cg) % fw == 0 and (z_blk * cg) % fw == 0
    cc, sc = _dft_cos_sin(cg)
    ct, st = _dft_cos_sin(t)
    consts = [jnp.asarray(m, F32).astype(BF16) for m in (cc, sc, ct, st)]
    body = functools.partial(_four_ctx_body, scale=float(1.0 / np.sqrt(t * cg)))
    sq = lambda n: pl.BlockSpec((n, n), lambda b: (0, 0))
    return pl.pallas_call(
        body,
        grid=(bsz,),
        in_specs=[pl.BlockSpec((None, t, fw), lambda b: (b, 0, u_blk * cg // fw)),
                  pl.BlockSpec((None, t, fw), lambda b: (b, 0, z_blk * cg // fw)),
                  sq(cg), sq(cg), sq(t), sq(t),
                  pl.BlockSpec((groups, cg, cg), lambda b: (0, 0, 0))],
        out_specs=pl.BlockSpec((None, t, fw), lambda b: (b, 0, 0)),
        out_shape=jax.ShapeDtypeStruct((bsz, t, fw), BF16),
        compiler_params=_params("arbitrary"),
        name="fourier_ctx",
    )(proj, proj, *consts, w4)


def _four_lat_body(u_ref, z_ref, cc_ref, sc_ref, wpq_ref, krc_ref, krs_ref, w4_ref, o_ref,
                   p_scr, q_scr, f_scr, *, scale, tile, oct_w):
    n_grp, rows, width, cg = p_scr.shape
    t = rows * width
    rows_per_tile = tile // width

    def width_stage(g, i):
        cols = slice(g * cg, (g + 1) * cg)
        x = u_ref[i * tile:(i + 1) * tile, cols]
        a = jnp.dot(x, cc_ref[...], preferred_element_type=F32).astype(BF16)
        b = jnp.dot(x, sc_ref[...], preferred_element_type=F32).astype(BF16)
        ab = jnp.concatenate([a, b], axis=0)
        pq = jnp.dot(wpq_ref[...], ab, preferred_element_type=F32)
        grid_rows = slice(i * rows_per_tile, (i + 1) * rows_per_tile)
        p_scr[g, grid_rows, :, :] = pq[:tile].reshape(rows_per_tile, width, cg)
        q_scr[g, grid_rows, :, :] = pq[tile:].reshape(rows_per_tile, width, cg)

    def row_stage(g, j):
        wcols = slice(j * oct_w, (j + 1) * oct_w)
        pj = p_scr[g, :, wcols, :].reshape(rows * oct_w, cg).astype(BF16)
        qj = q_scr[g, :, wcols, :].reshape(rows * oct_w, cg).astype(BF16)
        mixed = (jnp.dot(krc_ref[...], pj, preferred_element_type=F32)
                 - jnp.dot(krs_ref[...], qj, preferred_element_type=F32)) * scale
        f = jnp.dot(mixed.astype(BF16), w4_ref[g], preferred_element_type=F32)
        f_scr[g, :, wcols, :] = f.reshape(rows, oct_w, cg)

    def gate_stage(g, i):
        cols = slice(g * cg, (g + 1) * cg)
        tok = slice(i * tile, (i + 1) * tile)
        f = f_scr[g, i * rows_per_tile:(i + 1) * rows_per_tile, :, :].reshape(tile, cg)
        z = z_ref[tok, cols].astype(F32)
        o_ref[tok, cols] = (f * _silu(z)).astype(o_ref.dtype)

    for i in range(t // tile):
        for g in range(n_grp):
            width_stage(g, i)
    for j in range(width // oct_w):
        for g in range(n_grp):
            row_stage(g, j)
    for i in range(t // tile):
        for g in range(n_grp):
            gate_stage(g, i)


def _fourier_lat(proj, w4, groups, cg, u_blk, z_blk):
    bsz, t, _ = proj.shape
    width = GRID_W
    rows = t // width
    tile = 256
    oct_w = 8
    n_grp = 2
    assert groups % n_grp == 0 and u_blk % n_grp == 0 and z_blk % n_grp == 0
    gw = n_grp * cg
    cc, sc = _dft_cos_sin(cg)
    cw, sw = _dft_cos_sin(width)
    cr, sr = _dft_cos_sin(rows)
    eye_t = np.eye(tile // width)
    bwc, bws = np.kron(eye_t, cw), np.kron(eye_t, sw)
    wpq = np.block([[bwc, -bws], [bws, bwc]])
    eye_o = np.eye(oct_w)
    krc, krs = np.kron(cr, eye_o), np.kron(sr, eye_o)
    consts = [jnp.asarray(m, F32).astype(BF16) for m in (cc, sc, wpq, krc, krs)]
    body = functools.partial(_four_lat_body, scale=float(1.0 / np.sqrt(t * cg)), tile=tile, oct_w=oct_w)
    full = lambda a: pl.BlockSpec(a.shape, lambda b, g: (0, 0))
    return pl.pallas_call(
        body,
        grid=(bsz, groups // n_grp),
        in_specs=[pl.BlockSpec((None, t, gw), lambda b, g: (b, 0, u_blk // n_grp + g)),
                  pl.BlockSpec((None, t, gw), lambda b, g: (b, 0, z_blk // n_grp + g))]
                 + [full(a) for a in consts]
                 + [pl.BlockSpec((n_grp, cg, cg), lambda b, g: (g, 0, 0))],
        out_specs=pl.BlockSpec((None, t, gw), lambda b, g: (b, 0, g)),
        out_shape=jax.ShapeDtypeStruct((bsz, t, groups * cg), BF16),
        scratch_shapes=[pltpu.VMEM((n_grp, rows, width, cg), F32)] * 3,
        compiler_params=_params("arbitrary", "arbitrary"),
        name="fourier_lat",
    )(proj, proj, *consts, w4)


def _out_body(m_ref, f_ref, w_ref, x_ref, gate_ref, fw_ref, o_ref, *, tn):
    mw = m_ref.shape[1]
    d = o_ref.shape[1]
    m = m_ref[...]
    f = f_ref[...]
    ssq = jnp.zeros((o_ref.shape[0], 1), F32)
    for n in range(d // tn):
        cols = slice(n * tn, (n + 1) * tn)
        acc = (jnp.dot(m, w_ref[:mw, cols], preferred_element_type=F32)
               + jnp.dot(f, w_ref[mw:, cols], preferred_element_type=F32))
        y = x_ref[:, cols] + gate_ref[:, cols] * acc
        ssq = ssq + jnp.sum(y * y, axis=-1, keepdims=True)
        o_ref[:, cols] = y
    inv = lax.rsqrt(ssq * (1.0 / d) + EPS)
    for n in range(d // tn):
        cols = slice(n * tn, (n + 1) * tn)
        o_ref[:, cols] = o_ref[:, cols] * inv * fw_ref[:, cols]


def _out_proj(m_out, f_out, w_out, x, mod, mod_row, final_w):
    bsz, t, d = x.shape
    mw = m_out.shape[-1]
    tm = min(256, t)
    tiles = t // tm
    m2 = m_out.reshape(bsz * t, mw)
    f2 = f_out.reshape(bsz * t, f_out.shape[-1])
    x2 = x.reshape(bsz * t, d)
    y = pl.pallas_call(
        functools.partial(_out_body, tn=min(512, d)),
        grid=(bsz * tiles,),
        in_specs=[pl.BlockSpec((tm, mw), lambda i: (i, 0)),
                  pl.BlockSpec((tm, f2.shape[1]), lambda i: (i, 0)),
                  pl.BlockSpec(w_out.shape, lambda i: (0, 0), pipeline_mode=pl.Buffered(1)),
                  pl.BlockSpec((tm, d), lambda i: (i, 0)),
                  pl.BlockSpec((None, 1, d), lambda i: (mod_row(i // tiles) * 3 + 2, 0, 0)),
                  pl.BlockSpec((1, d), lambda i: (0, 0))],
        out_specs=pl.BlockSpec((tm, d), lambda i: (i, 0)),
        out_shape=jax.ShapeDtypeStruct((bsz * t, d), F32),
        compiler_params=_params("arbitrary"),
        name="out_proj",
    )(m2, f2, w_out, x2, mod, final_w)
    return y.reshape(bsz, t, d)


def _layer(x, mod, mod_row, norm_w, w_main, wg, bg, hnorm_w, w4, w_out, final_w,
           init, want_state, latent, heads, groups):
    bsz, t, d = x.shape
    mw = d // 2
    dv = mw // heads
    dqk = dv // 2
    cg = (d - mw) // groups
    h, gc, gr = _prenorm(x, mod, mod_row, norm_w, wg, bg, heads, min(MLSTM_CHUNK, t))
    k_t, proj = _in_proj(h.reshape(bsz * t, d), w_main, heads * dqk)
    proj = proj.reshape(bsz, t, -1)
    res = _mlstm(proj, k_t, gc, gr, hnorm_w, init, heads, dqk, dv, want_state)
    u_blk = 3 * mw // cg
    z_blk = (3 * mw + (d - mw)) // cg
    four = _fourier_lat if latent else _fourier_ctx
    f_out = four(proj, w4, groups, cg, u_blk, z_blk)
    y = _out_proj(res[0], f_out, w_out, x, mod, mod_row, final_w)
    return y, res[1:]


def kernel(x_prompt, x_sample, c, state_C, state_n, state_m, c_ctx, w_ada, b_ada, norm_w, w_in, b_gates, hnorm_w, w_four, w_out, final_norm_w):
    depth = w_ada.shape[0]
    assert depth == 1, "single-layer step only"
    bp, _, d = x_prompt.shape
    bs = x_sample.shape[0]
    heads = state_C.shape[3]
    groups = w_four.shape[1]
    mw = d // 2
    n_gates = 4 * heads
    gate0 = 4 * mw
    gw = heads * GATE_SLOTS
    assert 1 + bs <= MOD_ROWS and gw <= LANES

    cvec = jnp.zeros((MOD_ROWS, d), F32).at[0].set(c_ctx).at[1:1 + bs].set(c)
    mod = _modulation(cvec, w_ada[0], b_ada).reshape(MOD_ROWS * 3, 1, d)

    w_t = jnp.swapaxes(w_in, 1, 2)[0]
    w_main = _prep_w_in(w_t, gate0, n_gates, mw // 2)
    w_g = w_t[gate0:gate0 + n_gates].T.reshape(d, 4, heads).transpose(0, 2, 1)
    w_g = jnp.pad(w_g, ((0, 0), (0, 0), (0, GATE_SLOTS - 4))).reshape(d, gw)
    wg = jnp.pad(w_g, ((0, 0), (0, LANES - gw))).astype(BF16)
    b_g = jnp.pad(b_gates[0].reshape(4, heads).T, ((0, 0), (0, GATE_SLOTS - 4))).reshape(1, gw)
    bg = jnp.pad(b_g, ((0, 0), (0, LANES - gw))).astype(F32)
    w4 = w_four[0].astype(BF16)
    wo = w_out[0].astype(BF16)
    shared = (norm_w, w_main, wg, bg, hnorm_w, w4, wo, final_norm_w.reshape(1, d))

    y_prompt, (c_new, n_new, m_new) = _layer(
        x_prompt, mod, lambda b: b * 0, *shared, None, True, False, heads, groups)
    init = (state_C[:, 0], state_n[:, 0][..., None], state_m[:, 0][..., None, None])
    y_sample, _ = _layer(
        x_sample, mod, lambda b: b + 1, *shared, init, False, True, heads, groups)

    new_c = c_new[:, None]
    new_n = n_new.reshape(bp, 1, 2, heads, -1)
    new_m = m_new.reshape(bp, 1, 2, heads)
    return (y_prompt, y_sample, new_c, new_n, new_m)
```
